```python
import jax, jax.numpy as jnp
from jax import lax
import numpy as np

D_MODEL = 1024
BATCH = 8
SEQ = 2048
DEPTH = 4
DEC_BATCH = 128
DEC_SEQ = 8
PAST_LEN = 16384
PAGE_SIZE = 128

N_MIXERS = 2
N_A_LAYERS = (DEPTH + 1) // 2
N_B_LAYERS = DEPTH // 2
CONV_WIDTH = 31
POOL_WINDOWS = (2, 4, 8, 16)
POOL_GROUPS = len(POOL_WINDOWS)
POOL_GROUP_DIM = D_MODEL // POOL_GROUPS
POOL_PREFIX = max(POOL_WINDOWS) - 1
N_MEM = 256
N_MEM_HEADS = 4
MEM_HEAD_DIM = D_MODEL // N_MEM_HEADS
D_FF = 2816
FFN_CONV_WIDTH = 3
N_NORMS = 7
RMS_EPS = 1e-6
LN_EPS = 1e-5

kernel_name = 'hybrid_conformer_pool_memxattn_decoder_step'


def _rmsnorm(x, g):
    xf = x.astype(jnp.float32)
    y = xf * lax.rsqrt(jnp.mean(xf * xf, axis=-1, keepdims=True) + RMS_EPS)
    return (y * g.astype(jnp.float32)).astype(x.dtype)


def _layernorm(x, g, b):
    xf = x.astype(jnp.float32)
    mu = jnp.mean(xf, axis=-1, keepdims=True)
    xc = xf - mu
    y = xc * lax.rsqrt(jnp.mean(xc * xc, axis=-1, keepdims=True) + LN_EPS)
    return (y * g.astype(jnp.float32) + b.astype(jnp.float32)).astype(x.dtype)


def _causal_dwconv(ext, w, b):
    c = ext.shape[-1]
    y = lax.conv_general_dilated(ext, w[:, None, :].astype(ext.dtype), (1,), 'VALID',
                                 dimension_numbers=('NWC', 'WIO', 'NWC'),
                                 feature_group_count=c)
    return y + b.astype(y.dtype)


def _conformer_conv(h, prefix, w_in, b_in, w_dw, b_dw, ln_g, ln_b, w_out, b_out):
    u = h @ w_in + b_in
    a, gate = jnp.split(u, 2, axis=-1)
    glu = a * jax.nn.sigmoid(gate)
    ext = jnp.concatenate([prefix.astype(glu.dtype), glu], axis=1)
    c = _causal_dwconv(ext, w_dw, b_dw)
    c = jax.nn.silu(_layernorm(c, ln_g, ln_b))
    out = c @ w_out + b_out
    return out, ext[:, -(CONV_WIDTH - 1):]


def _multiscale_pool(h, prefix, pos0, w_group, scale):
    L = h.shape[1]
    z_in = jnp.concatenate([prefix.astype(h.dtype), h], axis=1)
    zf = z_in.astype(jnp.float32)
    csum = jnp.cumsum(zf, axis=1)
    csum = jnp.concatenate([jnp.zeros_like(csum[:, :1]), csum], axis=1)
    pos = (pos0 + jnp.arange(L)).astype(jnp.float32)
    off = POOL_PREFIX + 1
    means = []
    for g, w in enumerate(POOL_WINDOWS):
        sl = slice(g * POOL_GROUP_DIM, (g + 1) * POOL_GROUP_DIM)
        s = csum[:, off:off + L, sl] - csum[:, off - w:off - w + L, sl]
        cnt = jnp.minimum(jnp.float32(w), pos + 1.0)
        means.append(s / cnt[None, :, None])
    pooled = (jnp.concatenate(means, axis=-1) - zf[:, POOL_PREFIX:]).astype(h.dtype)
    b = h.shape[0]
    pg = pooled.reshape(b, L, POOL_GROUPS, POOL_GROUP_DIM)
    out = jnp.einsum('blgc,gcd->blgd', pg, w_group).reshape(b, L, D_MODEL) * scale
    return out, z_in[:, -POOL_PREFIX:]


def _mem_kv(mem, g_mem, w_kv):
    b = mem.shape[0]
    kv = (_rmsnorm(mem, g_mem) @ w_kv).reshape(b, N_MEM, 2, N_MEM_HEADS, MEM_HEAD_DIM)
    return kv[:, :, 0], kv[:, :, 1]


def _cross_attn(h, k, v, w_q, w_o):
    b, L, _ = h.shape
    q = (h @ w_q).reshape(b, L, N_MEM_HEADS, MEM_HEAD_DIM)
    s = jnp.einsum('blhd,bmhd->bhlm', q, k.astype(q.dtype)).astype(jnp.float32) * (MEM_HEAD_DIM ** -0.5)
    p = jax.nn.softmax(s, axis=-1).astype(v.dtype)
    o = jnp.einsum('bhlm,bmhd->blhd', p, v).reshape(b, L, D_MODEL)
    return o.astype(h.dtype) @ w_o


def _conv_ffn(h, prefix, w_up, w_dw, b_dw, w_down):
    u = h @ w_up
    ext = jnp.concatenate([prefix.astype(u.dtype), u], axis=1)
    c = _causal_dwconv(ext, w_dw, b_dw)
    g, val = jnp.split(c, 2, axis=-1)
    out = (jax.nn.silu(g) * val) @ w_down
    return out, ext[:, -(FFN_CONV_WIDTH - 1):]


def _trunk(x, pos0, conv_prefix, pool_prefix, ffn_prefix, mem_k, mem_v, norm_gains,
           a_w_in, a_b_in, a_w_dw, a_b_dw, a_ln_g, a_ln_b, a_w_out, a_b_out,
           p_w_group, p_scale, c_w_q, c_w_o, f_w_up, f_w_dw, f_b_dw, f_w_down):
    conv_states, pool_states, ffn_states = [], [], []
    for i in range(DEPTH):
        g = norm_gains[i]
        j = i // N_MIXERS
        h = _rmsnorm(x, g[0])
        if i % N_MIXERS == 0:
            out, st = _conformer_conv(h, conv_prefix[j], a_w_in[j], a_b_in[j], a_w_dw[j], a_b_dw[j],
                                      a_ln_g[j], a_ln_b[j], a_w_out[j], a_b_out[j])
            conv_states.append(st)
        else:
            out, st = _multiscale_pool(h, pool_prefix[j], pos0, p_w_group[j], p_scale[j])
            pool_states.append(st)
        x = x + _rmsnorm(out, g[1])
        h = _rmsnorm(x, g[2])
        x = x + _rmsnorm(_cross_attn(h, mem_k[i], mem_v[i], c_w_q[i], c_w_o[i]), g[3])
        h = _rmsnorm(x, g[4])
        out, st = _conv_ffn(h, ffn_prefix[i], f_w_up[i], f_w_dw[i], f_b_dw[i], f_w_down[i])
        ffn_states.append(st)
        x = x + _rmsnorm(out, g[5])
    return x, jnp.stack(conv_states), jnp.stack(pool_states), jnp.stack(ffn_states)


def setup_inputs(seed: int = 0) -> dict:
    key = jax.random.key(seed)
    ks = iter(jax.random.split(key, 40))

    def nrm(shape, scale=1.0):
        return jax.random.normal(next(ks), shape, jnp.float32) * scale

    d, f2 = D_MODEL, 2 * D_FF
    return {
        'x_prompt': nrm((BATCH, SEQ, d)),
        'x_sample': nrm((DEC_BATCH, DEC_SEQ, d)),
        'mem_prompt': nrm((BATCH, N_MEM, d)),
        'state_conv': nrm((N_A_LAYERS, DEC_BATCH, CONV_WIDTH - 1, d), 0.5),
        'state_pool': nrm((N_B_LAYERS, DEC_BATCH, POOL_PREFIX, d)),
        'state_ffn': nrm((DEPTH, DEC_BATCH, FFN_CONV_WIDTH - 1, f2)),
        'cache_mem_k': nrm((DEPTH, DEC_BATCH, N_MEM, N_MEM_HEADS, MEM_HEAD_DIM)),
        'cache_mem_v': nrm((DEPTH, DEC_BATCH, N_MEM, N_MEM_HEADS, MEM_HEAD_DIM)),
        'norm_gains': 1.0 + nrm((DEPTH, N_NORMS, d), 0.05),
        'a_w_in': nrm((N_A_LAYERS, d, 2 * d), d ** -0.5),
        'a_b_in': nrm((N_A_LAYERS, 2 * d), 0.02),
        'a_w_dw': nrm((N_A_LAYERS, CONV_WIDTH, d), CONV_WIDTH ** -0.5),
        'a_b_dw': nrm((N_A_LAYERS, d), 0.02),
        'a_ln_g': 1.0 + nrm((N_A_LAYERS, d), 0.05),
        'a_ln_b': nrm((N_A_LAYERS, d), 0.02),
        'a_w_out': nrm((N_A_LAYERS, d, d), d ** -0.5),
        'a_b_out': nrm((N_A_LAYERS, d), 0.02),
        'p_w_group': nrm((N_B_LAYERS, POOL_GROUPS, POOL_GROUP_DIM, POOL_GROUP_DIM), POOL_GROUP_DIM ** -0.5),
        'p_scale': 1.0 + nrm((N_B_LAYERS, d), 0.1),
        'c_w_q': nrm((DEPTH, d, d), d ** -0.5),
        'c_w_kv': nrm((DEPTH, d, 2 * d), d ** -0.5),
        'c_w_o': nrm((DEPTH, d, d), d ** -0.5),
        'f_w_up': nrm((DEPTH, d, f2), d ** -0.5),
        'f_w_dw': nrm((DEPTH, FFN_CONV_WIDTH, f2), FFN_CONV_WIDTH ** -0.5),
        'f_b_dw': nrm((DEPTH, f2), 0.02),
        'f_w_down': nrm((DEPTH, D_FF, d), D_FF ** -0.5),
    }


def reference(x_prompt, x_sample, mem_prompt, state_conv, state_pool, state_ffn, cache_mem_k, cache_mem_v,
              norm_gains, a_w_in, a_b_in, a_w_dw, a_b_dw, a_ln_g, a_ln_b, a_w_out, a_b_out,
              p_w_group, p_scale, c_w_q, c_w_kv, c_w_o, f_w_up, f_w_dw, f_b_dw, f_w_down):
    b = x_prompt.shape[0]
    mks, mvs = [], []
    for i in range(DEPTH):
        k, v = _mem_kv(mem_prompt, norm_gains[i, N_NORMS - 1], c_w_kv[i])
        mks.append(k)
        mvs.append(v)
    mem_k_prompt = jnp.stack(mks)
    mem_v_prompt = jnp.stack(mvs)
    zc = jnp.zeros((N_A_LAYERS, b, CONV_WIDTH - 1, D_MODEL), x_prompt.dtype)
    zp = jnp.zeros((N_B_LAYERS, b, POOL_PREFIX, D_MODEL), x_prompt.dtype)
    zf = jnp.zeros((DEPTH, b, FFN_CONV_WIDTH - 1, 2 * D_FF), x_prompt.dtype)
    y_prompt, conv_p, pool_p, ffn_p = _trunk(
        x_prompt, 0, zc, zp, zf, mem_k_prompt, mem_v_prompt, norm_gains,
        a_w_in, a_b_in, a_w_dw, a_b_dw, a_ln_g, a_ln_b, a_w_out, a_b_out,
        p_w_group, p_scale, c_w_q, c_w_o, f_w_up, f_w_dw, f_b_dw, f_w_down)
    y_sample, conv_s, pool_s, ffn_s = _trunk(
        x_sample, PAST_LEN, state_conv, state_pool, state_ffn, cache_mem_k, cache_mem_v, norm_gains,
        a_w_in, a_b_in, a_w_dw, a_b_dw, a_ln_g, a_ln_b, a_w_out, a_b_out,
        p_w_group, p_scale, c_w_q, c_w_o, f_w_up, f_w_dw, f_b_dw, f_w_down)
    return (y_prompt, y_sample, conv_p, pool_p, ffn_p, mem_k_prompt, mem_v_prompt, conv_s, pool_s, ffn_s)
```

```python
import functools

import jax
import jax.numpy as jnp
from jax import lax
from jax.experimental import pallas as pl
from jax.experimental.pallas import tpu as pltpu

D_MODEL = 1024
DEPTH = 4
PAST_LEN = 16384
CONV_WIDTH = 31
CONV_PREFIX = CONV_WIDTH - 1
POOL_WINDOWS = (2, 4, 8, 16)
POOL_GROUP_DIM = D_MODEL // len(POOL_WINDOWS)
POOL_PREFIX = max(POOL_WINDOWS) - 1
N_MEM = 256
N_MEM_HEADS = 4
MEM_HEAD_DIM = D_MODEL // N_MEM_HEADS
D_FF = 2816
FFN_CONV_WIDTH = 3
FFN_PREFIX = FFN_CONV_WIDTH - 1
N_NORMS = 7
RMS_EPS = 1e-6
LN_EPS = 1e-5

SUBLANES = 8
CONV_PAD = 32
POOL_PAD = 16
FFN_PAD = 8
FFN_CHUNK = 256
VMEM_LIMIT_BYTES = 56 * 1024 * 1024

BF16 = jnp.bfloat16
F32 = jnp.float32


def _rms(x, g):
    ms = jnp.mean(x * x, axis=-1, keepdims=True)
    return x * lax.rsqrt(ms + RMS_EPS) * g


def _bdot(a, w):
    return jnp.dot(a.astype(BF16), w, preferred_element_type=F32)


def _sigmoid(x):
    return 1.0 / (1.0 + jnp.exp(-x))


def _row_chunk_loop(nb, length, rc, body):
    nch = length // rc

    def step(idx, carry):
        if nch == 1:
            n, r0 = idx, 0
        else:
            n = idx // nch
            r0 = pl.multiple_of((idx % nch) * rc, rc)
        body(n, r0)
        return carry

    lax.fori_loop(0, nb * nch, step, 0)


def _conformer_kernel(has_prefix, rc, *refs):
    if has_prefix:
        (x_ref, pre_ref, g_ref, w_in, b_in, w_dw, b_dw, ln_g, ln_b, w_out, b_out,
         y_ref, st_ref, ext, cbuf, stage) = refs
    else:
        (x_ref, g_ref, w_in, b_in, w_dw, b_dw, ln_g, ln_b, w_out, b_out,
         y_ref, st_ref, ext, cbuf, stage) = refs
    t = pl.program_id(1)
    nt = pl.num_programs(1)
    nb, length, c = x_ref.shape
    rows = nb * length
    off = CONV_PAD - CONV_PREFIX

    x = x_ref[...].reshape(rows, c)
    h = _rms(x, g_ref[0:1, :])
    u = _bdot(h, w_in[...]) + b_in[...]
    glu = u[:, :c] * _sigmoid(u[:, c:])

    @pl.when(t == 0)
    def _():
        if has_prefix:
            ext[:, off:CONV_PAD, :] = pre_ref[...]
        else:
            ext[:, 0:CONV_PAD, :] = jnp.zeros((nb, CONV_PAD, c), F32)

    ext[:, CONV_PAD:CONV_PAD + length, :] = glu.reshape(nb, length, c)

    def chunk(n, r0):
        stage[...] = ext[n, pl.ds(r0, rc + CONV_PAD), :]
        acc = jnp.broadcast_to(b_dw[...], (rc, c))
        for k in range(CONV_WIDTH):
            acc = acc + w_dw[k:k + 1, :] * stage[off + k:off + k + rc, :]
        mu = jnp.mean(acc, axis=-1, keepdims=True)
        xc = acc - mu
        var = jnp.mean(xc * xc, axis=-1, keepdims=True)
        yn = xc * lax.rsqrt(var + LN_EPS) * ln_g[...] + ln_b[...]
        cbuf[pl.ds(pl.multiple_of(n * length + r0, SUBLANES), rc), :] = yn * _sigmoid(yn)

    _row_chunk_loop(nb, length, rc, chunk)

    out = _bdot(cbuf[...], w_out[...]) + b_out[...]
    y_ref[...] = (x + _rms(out, g_ref[1:2, :])).reshape(nb, length, c)

    @pl.when(t == nt - 1)
    def _():
        st_ref[...] = ext[:, length + off:length + CONV_PAD, :]

    if not has_prefix:
        ext[:, 0:CONV_PAD, :] = ext[:, length:length + CONV_PAD, :]


def _conformer(x, prefix, j, gains, i, w_in, b_in, w_dw, b_dw, ln_g, ln_b, w_out, b_out, nb, length, rc):
    b, s, c = x.shape
    has_prefix = prefix is not None
    if has_prefix:
        assert s == length
    grid = (b // nb, s // length)
    lay = lambda bb, tt: (i, 0, 0)
    layj = lambda bb, tt: (j, 0, 0)
    in_specs = [pl.BlockSpec((nb, length, c), lambda bb, tt: (bb, tt, 0))]
    args = [x]
    if has_prefix:
        in_specs.append(pl.BlockSpec((None, nb, CONV_PREFIX, c), lambda bb, tt: (j, bb, 0, 0)))
        args.append(prefix)
    in_specs += [
        pl.BlockSpec((None, N_NORMS, c), lay),
        pl.BlockSpec((None, c, 2 * c), layj),
        pl.BlockSpec((None, 1, 2 * c), layj),
        pl.BlockSpec((None, CONV_WIDTH, c), layj),
        pl.BlockSpec((None, 1, c), layj),
        pl.BlockSpec((None, 1, c), layj),
        pl.BlockSpec((None, 1, c), layj),
        pl.BlockSpec((None, c, c), layj),
        pl.BlockSpec((None, 1, c), layj),
    ]
    args += [gains, w_in, b_in, w_dw, b_dw, ln_g, ln_b, w_out, b_out]
    return pl.pallas_call(
        functools.partial(_conformer_kernel, has_prefix, rc),
        grid=grid,
        in_specs=in_specs,
        out_specs=[
            pl.BlockSpec((nb, length, c), lambda bb, tt: (bb, tt, 0)),
            pl.BlockSpec((nb, CONV_PREFIX, c), lambda bb, tt: (bb, 0, 0)),
        ],
        out_shape=[
            jax.ShapeDtypeStruct((b, s, c), F32),
            jax.ShapeDtypeStruct((b, CONV_PREFIX, c), F32),
        ],
        scratch_shapes=[
            pltpu.VMEM((nb, CONV_PAD + length, c), F32),
            pltpu.VMEM((nb * length, c), F32),
            pltpu.VMEM((rc + CONV_PAD, c), F32),
        ],
        compiler_params=pltpu.CompilerParams(
            dimension_semantics=("arbitrary", "arbitrary"),
            vmem_limit_bytes=VMEM_LIMIT_BYTES),
        name="conformer_mixer",
    )(*args)


def _pool_kernel(has_prefix, rc, pos0, *refs):
    if has_prefix:
        x_ref, pre_ref, g_ref, w_grp, scale, y_ref, st_ref, ext, pbuf, stage = refs
    else:
        x_ref, g_ref, w_grp, scale, y_ref, st_ref, ext, pbuf, stage = refs
    t = pl.program_id(1)
    nt = pl.num_programs(1)
    nb, length, c = x_ref.shape
    rows = nb * length
    off = POOL_PAD - POOL_PREFIX
    gd = POOL_GROUP_DIM

    x = x_ref[...].reshape(rows, c)
    h = _rms(x, g_ref[0:1, :])

    @pl.when(t == 0)
    def _():
        if has_prefix:
            ext[:, off:POOL_PAD, :] = pre_ref[...]
        else:
            ext[:, 0:POOL_PAD, :] = jnp.zeros((nb, POOL_PAD, c), F32)

    ext[:, POOL_PAD:POOL_PAD + length, :] = h.reshape(nb, length, c)

    def chunk(n, r0):
        pos = (pos0 + t * length + r0 + lax.broadcasted_iota(jnp.int32, (rc, 1), 0)).astype(F32)
        stage[...] = ext[n, pl.ds(r0, rc + POOL_PAD), :]
        for g, w in enumerate(POOL_WINDOWS):
            cols = slice(g * gd, (g + 1) * gd)
            zt = stage[POOL_PAD:POOL_PAD + rc, cols]
            s = zt
            for d in range(1, w):
                s = s + stage[POOL_PAD - d:POOL_PAD - d + rc, cols]
            cnt = jnp.minimum(jnp.float32(w), pos + 1.0)
            pbuf[pl.ds(pl.multiple_of(n * length + r0, SUBLANES), rc), cols] = s / cnt - zt

    _row_chunk_loop(nb, length, rc, chunk)

    outs = [_bdot(pbuf[:, g * gd:(g + 1) * gd], w_grp[g]) for g in range(len(POOL_WINDOWS))]
    out = jnp.concatenate(outs, axis=-1) * scale[...]
    y_ref[...] = (x + _rms(out, g_ref[1:2, :])).reshape(nb, length, c)

    @pl.when(t == nt - 1)
    def _():
        st_ref[...] = ext[:, length + off:length + POOL_PAD, :]

    if not has_prefix:
        ext[:, 0:POOL_PAD, :] = ext[:, length:length + POOL_PAD, :]


def _pool(x, prefix, j, pos0, gains, i, w_grp, scale, nb, length, rc):
    b, s, c = x.shape
    has_prefix = prefix is not None
    if has_prefix:
        assert s == length
    grid = (b // nb, s // length)
    in_specs = [pl.BlockSpec((nb, length, c), lambda bb, tt: (bb, tt, 0))]
    args = [x]
    if has_prefix:
        in_specs.append(pl.BlockSpec((None, nb, POOL_PREFIX, c), lambda bb, tt: (j, bb, 0, 0)))
        args.append(prefix)
    in_specs += [
        pl.BlockSpec((None, N_NORMS, c), lambda bb, tt: (i, 0, 0)),
        pl.BlockSpec((None, len(POOL_WINDOWS), POOL_GROUP_DIM, POOL_GROUP_DIM), lambda bb, tt: (j, 0, 0, 0)),
        pl.BlockSpec((None, 1, c), lambda bb, tt: (j, 0, 0)),
    ]
    args += [gains, w_grp, scale]
    return pl.pallas_call(
        functools.partial(_pool_kernel, has_prefix, rc, pos0),
        grid=grid,
        in_specs=in_specs,
        out_specs=[
            pl.BlockSpec((nb, length, c), lambda bb, tt: (bb, tt, 0)),
            pl.BlockSpec((nb, POOL_PREFIX, c), lambda bb, tt: (bb, 0, 0)),
        ],
        out_shape=[
            jax.ShapeDtypeStruct((b, s, c), F32),
            jax.ShapeDtypeStruct((b, POOL_PREFIX, c), F32),
        ],
        scratch_shapes=[
            pltpu.VMEM((nb, POOL_PAD + length, c), F32),
            pltpu.VMEM((nb * length, c), F32),
            pltpu.VMEM((rc + POOL_PAD, c), F32),
        ],
        compiler_params=pltpu.CompilerParams(
            dimension_semantics=("arbitrary", "arbitrary"),
            vmem_limit_bytes=VMEM_LIMIT_BYTES),
        name="pool_mixer",
    )(*args)


def _attn_kernel(x_ref, k_ref, v_ref, g_ref, w_q, w_o, y_ref, obuf):
    nb, length, c = x_ref.shape
    rows = nb * length
    hd = MEM_HEAD_DIM
    x = x_ref[...].reshape(rows, c)
    h = _rms(x, g_ref[2:3, :])
    q = (_bdot(h, w_q[...]) * (hd ** -0.5)).astype(BF16)
    for n in range(nb):
        qn = q[n * length:(n + 1) * length]
        for hh in range(N_MEM_HEADS):
            cols = slice(hh * hd, (hh + 1) * hd)
            kh = k_ref[n, :, cols].astype(BF16)
            vh = v_ref[n, :, cols].astype(BF16)
            s = lax.dot_general(qn[:, cols], kh, (((1,), (1,)), ((), ())), preferred_element_type=F32)
            e = jnp.exp(s - jnp.max(s, axis=-1, keepdims=True))
            p = e / jnp.sum(e, axis=-1, keepdims=True)
            obuf[n * length:(n + 1) * length, cols] = jnp.dot(p.astype(BF16), vh, preferred_element_type=F32)
    out = _bdot(obuf[...], w_o[...])
    y_ref[...] = (x + _rms(out, g_ref[3:4, :])).reshape(nb, length, c)


def _attn(x, mem_k, mem_v, gains, i, w_q, w_o, nb, length):
    b, s, c = x.shape
    grid = (b // nb, s // length)
    return pl.pallas_call(
        _attn_kernel,
        grid=grid,
        in_specs=[
            pl.BlockSpec((nb, length, c), lambda bb, tt: (bb, tt, 0)),
            pl.BlockSpec((None, nb, N_MEM, c), lambda bb, tt: (i, bb, 0, 0)),
            pl.BlockSpec((None, nb, N_MEM, c), lambda bb, tt: (i, bb, 0, 0)),
            pl.BlockSpec((None, N_NORMS, c), lambda bb, tt: (i, 0, 0)),
            pl.BlockSpec((None, c, c), lambda bb, tt: (i, 0, 0)),
            pl.BlockSpec((None, c, c), lambda bb, tt: (i, 0, 0)),
        ],
        out_specs=pl.BlockSpec((nb, length, c), lambda bb, tt: (bb, tt, 0)),
        out_shape=jax.ShapeDtypeStruct((b, s, c), F32),
        scratch_shapes=[pltpu.VMEM((nb * length, c), F32)],
        compiler_params=pltpu.CompilerParams(
            dimension_semantics=("arbitrary", "arbitrary"),
            vmem_limit_bytes=VMEM_LIMIT_BYTES),
        name="mem_cross_attn",
    )(x, mem_k, mem_v, gains, w_q, w_o)


def _ffn_kernel(has_prefix, *refs):
    if has_prefix:
        x_ref, pre_ref, g_ref, w_up, w_dw, b_dw, w_down, y_ref, st_ref, ext = refs
    else:
        x_ref, g_ref, w_up, w_dw, b_dw, w_down, y_ref, st_ref, ext = refs
    t = pl.program_id(1)
    nt = pl.num_programs(1)
    nb, length, c = x_ref.shape
    rows = nb * length
    f2 = 2 * D_FF
    off = FFN_PAD - FFN_PREFIX
    cw = FFN_CHUNK

    x = x_ref[...].reshape(rows, c)
    hb = _rms(x, g_ref[4:5, :]).astype(BF16)

    @pl.when(t == 0)
    def _():
        if has_prefix:
            ext[:, off:FFN_PAD, :] = pre_ref[...]
        else:
            ext[:, 0:FFN_PAD, :] = jnp.zeros((nb, FFN_PAD, f2), F32)

    def conv(u, cols):
        ext[:, FFN_PAD:FFN_PAD + length, cols] = u.reshape(nb, length, cw)
        y = b_dw[:, cols] + w_dw[2:3, cols] * u
        for k in range(FFN_CONV_WIDTH - 1):
            tap = ext[:, off + k:off + k + length, cols].reshape(rows, cw)
            y = y + w_dw[k:k + 1, cols] * tap
        return y

    acc = jnp.zeros((rows, c), F32)
    for ci in range(D_FF // cw):
        gcols = slice(ci * cw, (ci + 1) * cw)
        vcols = slice(D_FF + ci * cw, D_FF + (ci + 1) * cw)
        cg = conv(jnp.dot(hb, w_up[:, gcols], preferred_element_type=F32), gcols)
        cv = conv(jnp.dot(hb, w_up[:, vcols], preferred_element_type=F32), vcols)
        act = cg * _sigmoid(cg) * cv
        acc = acc + _bdot(act, w_down[gcols, :])
    y_ref[...] = (x + _rms(acc, g_ref[5:6, :])).reshape(nb, length, c)

    @pl.when(t == nt - 1)
    def _():
        st_ref[...] = ext[:, length + off:length + FFN_PAD, :]

    if not has_prefix:
        ext[:, 0:FFN_PAD, :] = ext[:, length:length + FFN_PAD, :]


def _ffn(x, prefix, gains, i, w_up, w_dw, b_dw, w_down, nb, length):
    b, s, c = x.shape
    f2 = 2 * D_FF
    has_prefix = prefix is not None
    if has_prefix:
        assert s == length
    grid = (b // nb, s // length)
    lay = lambda bb, tt: (i, 0, 0)
    single = pl.Buffered(1)
    in_specs = [pl.BlockSpec((nb, length, c), lambda bb, tt: (bb, tt, 0))]
    args = [x]
    if has_prefix:
        in_specs.append(pl.BlockSpec((None, nb, FFN_PREFIX, f2), lambda bb, tt: (i, bb, 0, 0)))
        args.append(prefix)
    in_specs += [
        pl.BlockSpec((None, N_NORMS, c), lay),
        pl.BlockSpec((None, c, f2), lay, pipeline_mode=single),
        pl.BlockSpec((None, FFN_CONV_WIDTH, f2), lay),
        pl.BlockSpec((None, 1, f2), lay),
        pl.BlockSpec((None, D_FF, c), lay, pipeline_mode=single),
    ]
    args += [gains, w_up, w_dw, b_dw, w_down]
    return pl.pallas_call(
        functools.partial(_ffn_kernel, has_prefix),
        grid=grid,
        in_specs=in_specs,
        out_specs=[
            pl.BlockSpec((nb, length, c), lambda bb, tt: (bb, tt, 0)),
            pl.BlockSpec((nb, FFN_PREFIX, f2), lambda bb, tt: (bb, 0, 0)),
        ],
        out_shape=[
            jax.ShapeDtypeStruct((b, s, c), F32),
            jax.ShapeDtypeStruct((b, FFN_PREFIX, f2), F32),
        ],
        scratch_shapes=[pltpu.VMEM((nb, FFN_PAD + length, f2), F32)],
        compiler_params=pltpu.CompilerParams(
            dimension_semantics=("arbitrary", "arbitrary"),
            vmem_limit_bytes=VMEM_LIMIT_BYTES),
        name="conv_ffn",
    )(*args)


def _mem_kv_kernel(m_ref, g_ref, w_kv, k_ref, v_ref):
    c = m_ref.shape[-1]
    kv = _bdot(_rms(m_ref[...], g_ref[N_NORMS - 1:N_NORMS, :]), w_kv[...])
    k_ref[...] = kv[:, :c]
    v_ref[...] = kv[:, c:]


def _mem_kv(mem, gains, w_kv, tile):
    r, c = mem.shape
    return pl.pallas_call(
        _mem_kv_kernel,
        grid=(DEPTH, r // tile),
        in_specs=[
            pl.BlockSpec((tile, c), lambda ii, rr: (rr, 0)),
            pl.BlockSpec((None, N_NORMS, c), lambda ii, rr: (ii, 0, 0)),
            pl.BlockSpec((None, c, 2 * c), lambda ii, rr: (ii, 0, 0)),
        ],
        out_specs=[
            pl.BlockSpec((None, tile, c), lambda ii, rr: (ii, rr, 0)),
            pl.BlockSpec((None, tile, c), lambda ii, rr: (ii, rr, 0)),
        ],
        out_shape=[
            jax.ShapeDtypeStruct((DEPTH, r, c), F32),
            jax.ShapeDtypeStruct((DEPTH, r, c), F32),
        ],
        compiler_params=pltpu.CompilerParams(
            dimension_semantics=("arbitrary", "arbitrary"),
            vmem_limit_bytes=VMEM_LIMIT_BYTES),
        name="mem_kv",
    )(mem, gains, w_kv)


_PROMPT_TILE = 512
_PLAN = {
    "prompt": dict(conf=(1, _PROMPT_TILE, 32), pool=(1, _PROMPT_TILE, 32), attn=(1, _PROMPT_TILE),
                   ffn=(1, _PROMPT_TILE)),
    "sample": dict(conf=(32, 8, 8), pool=(32, 8, 8), attn=(4, 8), ffn=(32, 8)),
}


def _trunk(x, pos0, plan, conv_pre, pool_pre, ffn_pre, mem_k, mem_v, p):
    conv_states, pool_states, ffn_states = [], [], []
    for i in range(DEPTH):
        j = i // 2
        if i % 2 == 0:
            x, st = _conformer(x, conv_pre, j, p["gains"], i, p["a_w_in"], p["a_b_in"], p["a_w_dw"], p["a_b_dw"],
                               p["a_ln_g"], p["a_ln_b"], p["a_w_out"], p["a_b_out"], *plan["conf"])
            conv_states.append(st)
        else:
            x, st = _pool(x, pool_pre, j, pos0, p["gains"], i, p["p_w_group"], p["p_scale"], *plan["pool"])
            pool_states.append(st)
        x = _attn(x, mem_k, mem_v, p["gains"], i, p["c_w_q"], p["c_w_o"], *plan["attn"])
        x, st = _ffn(x, ffn_pre, p["gains"], i, p["f_w_up"], p["f_w_dw"], p["f_b_dw"], p["f_w_down"], *plan["ffn"])
        ffn_states.append(st)
    return x, jnp.stack(conv_states), jnp.stack(pool_states), jnp.stack(ffn_states)


def kernel(x_prompt, x_sample, mem_prompt, state_conv, state_pool, state_ffn, cache_mem_k, cache_mem_v, norm_gains, a_w_in, a_b_in, a_w_dw, a_b_dw, a_ln_g, a_ln_b, a_w_out, a_b_out, p_w_group, p_scale, c_w_q, c_w_kv, c_w_o, f_w_up, f_w_dw, f_b_dw, f_w_down):
    b = x_prompt.shape[0]
    db = x_sample.shape[0]
    c = D_MODEL
    p = dict(
        gains=norm_gains,
        a_w_in=a_w_in.astype(BF16), a_b_in=a_b_in[:, None, :], a_w_dw=a_w_dw, a_b_dw=a_b_dw[:, None, :],
        a_ln_g=a_ln_g[:, None, :], a_ln_b=a_ln_b[:, None, :], a_w_out=a_w_out.astype(BF16),
        a_b_out=a_b_out[:, None, :],
        p_w_group=p_w_group.astype(BF16), p_scale=p_scale[:, None, :],
        c_w_q=c_w_q.astype(BF16), c_w_o=c_w_o.astype(BF16),
        f_w_up=f_w_up.astype(BF16), f_w_dw=f_w_dw, f_b_dw=f_b_dw[:, None, :], f_w_down=f_w_down.astype(BF16),
    )
    mk, mv = _mem_kv(mem_prompt.reshape(b * N_MEM, c), norm_gains, c_w_kv.astype(BF16), 512)
    mk = mk.reshape(DEPTH, b, N_MEM, c)
    mv = mv.reshape(DEPTH, b, N_MEM, c)
    y_p, conv_p, pool_p, ffn_p = _trunk(x_prompt, 0, _PLAN["prompt"], None, None, None, mk, mv, p)
    ck = cache_mem_k.reshape(DEPTH, db, N_MEM, c)
    cv = cache_mem_v.reshape(DEPTH, db, N_MEM, c)
    y_s, conv_s, pool_s, ffn_s = _trunk(x_sample, PAST_LEN, _PLAN["sample"], state_conv, state_pool, state_ffn,
                                        ck, cv, p)
    kv_shape = (DEPTH, b, N_MEM, N_MEM_HEADS, MEM_HEAD_DIM)
    return (y_p, y_s, conv_p, pool_p, ffn_p, mk.reshape(kv_shape), mv.reshape(kv_shape), conv_s, pool_s, ffn_s)
```

```python
import functools

import jax
import jax.numpy as jnp
from jax import lax
from jax.experimental import pallas as pl
from jax.experimental.pallas import tpu as pltpu

D_MODEL = 1024
DEPTH = 4
PAST_LEN = 16384
CONV_WIDTH = 31
CONV_PREFIX = CONV_WIDTH - 1
POOL_WINDOWS = (2, 4, 8, 16)
POOL_GROUP_DIM = D_MODEL // len(POOL_WINDOWS)
POOL_PREFIX = max(POOL_WINDOWS) - 1
N_MEM = 256
N_MEM_HEADS = 4
MEM_HEAD_DIM = D_MODEL // N_MEM_HEADS
D_FF = 2816
FFN_CONV_WIDTH = 3
FFN_PREFIX = FFN_CONV_WIDTH - 1
N_NORMS = 7
RMS_EPS = 1e-6
LN_EPS = 1e-5

SUBLANES = 8
LANES = 128
PROMPT_TILE = 512
PROMPT_SEG = PROMPT_TILE // SUBLANES
FFN_CHUNK = 256
VMEM_LIMIT_BYTES = 56 * 1024 * 1024

BF16 = jnp.bfloat16
F32 = jnp.float32


def _rms(x, g):
    ms = jnp.mean(x * x, axis=-1, keepdims=True)
    return x * lax.rsqrt(ms + RMS_EPS) * g


def _bdot(a, w):
    return jnp.dot(a.astype(BF16), w, preferred_element_type=F32)


def _sigmoid(x):
    return 1.0 / (1.0 + jnp.exp(-x))


def _to_segments(x):
    rows, c = x.shape
    return x.reshape(SUBLANES, rows // SUBLANES, c).swapaxes(0, 1).reshape(rows, c)


def _from_segments(x):
    rows, c = x.shape
    return x.reshape(rows // SUBLANES, SUBLANES, c).swapaxes(0, 1).reshape(rows, c)


def _load_rows(x_ref, sample, permute_in):
    if sample:
        return jnp.concatenate([x_ref[:, t, :] for t in range(x_ref.shape[1])], axis=0)
    x = x_ref[...]
    return _to_segments(x) if permute_in else x


def _store_rows(y_ref, y, sample, permute_out):
    if sample:
        nb = y_ref.shape[0]
        for t in range(y_ref.shape[1]):
            y_ref[:, t, :] = y[t * nb:(t + 1) * nb]
    else:
        y_ref[...] = _from_segments(y) if permute_out else y


def _segment_history(last, carry_val):
    sub = lax.broadcasted_iota(jnp.int32, last.shape, 1)
    return pltpu.roll(jnp.where(sub == SUBLANES - 1, carry_val, last), 1, axis=1)


def _fill_history(ext, cur3, hist, carry, pre_ref, t, sample):
    n_cur = cur3.shape[0]
    if sample:
        ext[0:hist] = pre_ref[...]
    else:
        @pl.when(t == 0)
        def _():
            carry[...] = jnp.zeros(carry.shape, F32)

        last = cur3[n_cur - hist:]
        ext[0:hist] = _segment_history(last, carry[...])
        carry[...] = last
    ext[hist:hist + n_cur] = cur3


def _store_state(st_ref, ext, hist, carry, t, nt, sample):
    n_cur = ext.shape[0] - hist
    if sample:
        st_ref[...] = ext[n_cur:n_cur + hist]
    else:
        @pl.when(t == nt - 1)
        def _():
            st_ref[...] = carry[:, SUBLANES - 1, :]


def _split_refs(refs, sample, n_in, n_out, n_scratch):
    refs = list(refs)
    x_ref = refs.pop(0)
    pre_ref = refs.pop(0) if sample else None
    ins, refs = refs[:n_in], refs[n_in:]
    outs, refs = refs[:n_out], refs[n_out:]
    scr, refs = refs[:n_scratch], refs[n_scratch:]
    carry = None if sample else refs.pop(0)
    assert not refs
    return x_ref, pre_ref, ins, outs, scr, carry


def _conformer_kernel(sample, permute_in, rc, *refs):
    x_ref, pre_ref, ins, outs, scr, carry = _split_refs(refs, sample, 9, 2, 2)
    g_ref, w_in, b_in, w_dw, b_dw, ln_g, ln_b, w_out, b_out = ins
    y_ref, st_ref = outs
    ext, cbuf = scr
    t = pl.program_id(1)
    nt = pl.num_programs(1)
    n_cur, r8, c = cbuf.shape
    rows = n_cur * r8

    x = _load_rows(x_ref, sample, permute_in)
    h = _rms(x, g_ref[0:1, :])
    u = _bdot(h, w_in[...]) + b_in[...]
    glu = u[:, :c] * _sigmoid(u[:, c:])
    _fill_history(ext, glu.reshape(n_cur, r8, c), CONV_PREFIX, carry, pre_ref, t, sample)

    def chunk(ci, carry_):
        i0 = ci * rc
        for g in range(r8 // SUBLANES):
            rsl = slice(g * SUBLANES, (g + 1) * SUBLANES)
            def lane_col(l, carry2):
                cols = pl.ds(pl.multiple_of(l * LANES, LANES), LANES)
                bias = jnp.broadcast_to(b_dw[:, cols], (SUBLANES, LANES))
                accs = [bias] * rc
                for m in range(rc + CONV_PREFIX):
                    tile = ext[i0 + m, rsl, cols]
                    for j in range(max(0, m - CONV_PREFIX), min(rc, m + 1)):
                        accs[j] = accs[j] + w_dw[m - j:m - j + 1, cols] * tile
                cbuf[pl.ds(i0, rc), rsl, cols] = jnp.stack(accs)
                return carry2

            lax.fori_loop(0, c // LANES, lane_col, 0)
        return carry_

    lax.fori_loop(0, n_cur // rc, chunk, 0)

    v = cbuf[...].reshape(rows, c)
    mu = jnp.mean(v, axis=-1, keepdims=True)
    xc = v - mu
    var = jnp.mean(xc * xc, axis=-1, keepdims=True)
    yn = xc * lax.rsqrt(var + LN_EPS) * ln_g[...] + ln_b[...]
    out = _bdot(yn * _sigmoid(yn), w_out[...]) + b_out[...]
    _store_rows(y_ref, x + _rms(out, g_ref[1:2, :]), sample, False)
    _store_state(st_ref, ext, CONV_PREFIX, carry, t, nt, sample)


def _x_specs(sample, nb, length, c):
    if sample:
        return pl.BlockSpec((nb, length, c), lambda bb, tt: (bb, 0, 0))
    return pl.BlockSpec((None, length, c), lambda bb, tt: (bb, tt, 0))


def _state_spec(sample, nb, hist, width, time_major=False):
    if sample and time_major:
        return pl.BlockSpec((hist, nb, width), lambda bb, tt: (0, bb, 0))
    if sample:
        return pl.BlockSpec((nb, hist, width), lambda bb, tt: (bb, 0, 0))
    return pl.BlockSpec((None, hist, width), lambda bb, tt: (bb, 0, 0))


def _state_shape(sample, b, hist, width, time_major=False):
    return jax.ShapeDtypeStruct((hist, b, width) if (sample and time_major) else (b, hist, width), F32)


def _time_tiles(sample, nb, length):
    return (length, nb) if sample else (length // SUBLANES, SUBLANES)


def _params(sem=("arbitrary", "arbitrary")):
    return pltpu.CompilerParams(dimension_semantics=sem, vmem_limit_bytes=VMEM_LIMIT_BYTES)


def _conformer(x, prefix, j, gains, i, w_in, b_in, w_dw, b_dw, ln_g, ln_b, w_out, b_out, nb, length, rc,
               permute_in):
    b, s, c = x.shape
    sample = prefix is not None
    assert (s == length) if sample else (nb == 1)
    n_cur, r8 = _time_tiles(sample, nb, length)
    lay = lambda bb, tt: (i, 0, 0)
    layj = lambda bb, tt: (j, 0, 0)
    single = pl.Buffered(1)
    in_specs = [_x_specs(sample, nb, length, c)]
    args = [x]
    if sample:
        in_specs.append(pl.BlockSpec((None, CONV_PREFIX, nb, c), lambda bb, tt: (j, 0, bb, 0)))
        args.append(prefix)
    in_specs += [
        pl.BlockSpec((None, N_NORMS, c), lay),
        pl.BlockSpec((None, c, 2 * c), layj, pipeline_mode=single),
        pl.BlockSpec((None, 1, 2 * c), layj),
        pl.BlockSpec((None, CONV_WIDTH, c), layj),
        pl.BlockSpec((None, 1, c), layj),
        pl.BlockSpec((None, 1, c), layj),
        pl.BlockSpec((None, 1, c), layj),
        pl.BlockSpec((None, c, c), layj, pipeline_mode=single),
        pl.BlockSpec((None, 1, c), layj),
    ]
    args += [gains, w_in, b_in, w_dw, b_dw, ln_g, ln_b, w_out, b_out]
    scratch = [pltpu.VMEM((CONV_PREFIX + n_cur, r8, c), F32), pltpu.VMEM((n_cur, r8, c), F32)]
    if not sample:
        scratch.append(pltpu.VMEM((CONV_PREFIX, SUBLANES, c), F32))
    return pl.pallas_call(
        functools.partial(_conformer_kernel, sample, permute_in, rc),
        grid=(b // nb, s // length),
        in_specs=in_specs,
        out_specs=[_x_specs(sample, nb, length, c), _state_spec(sample, nb, CONV_PREFIX, c, True)],
        out_shape=[jax.ShapeDtypeStruct((b, s, c), F32), _state_shape(sample, b, CONV_PREFIX, c, True)],
        scratch_shapes=scratch,
        compiler_params=_params(),
        name="conformer_mixer",
    )(*args)


def _pool_kernel(sample, rc, pos0, *refs):
    x_ref, pre_ref, ins, outs, scr, carry = _split_refs(refs, sample, 3, 2, 2)
    g_ref, w_grp, scale = ins
    y_ref, st_ref = outs
    ext, pbuf = scr
    t = pl.program_id(1)
    nt = pl.num_programs(1)
    n_cur, r8, c = pbuf.shape
    rows = n_cur * r8
    gd = POOL_GROUP_DIM
    hist = POOL_PREFIX

    x = _load_rows(x_ref, sample, False)
    h = _rms(x, g_ref[0:1, :])
    _fill_history(ext, h.reshape(n_cur, r8, c), hist, carry, pre_ref, t, sample)

    def chunk(ci, carry_):
        i0 = ci * rc
        step = i0 + lax.broadcasted_iota(jnp.int32, (rc, SUBLANES, 1), 0)
        if sample:
            pos = pos0 + step
        else:
            seg = lax.broadcasted_iota(jnp.int32, (rc, SUBLANES, 1), 1)
            pos = pos0 + t * rows + seg * n_cur + step
        posf = pos.astype(F32)
        for g in range(r8 // SUBLANES):
            rsl = slice(g * SUBLANES, (g + 1) * SUBLANES)
            for gi, w in enumerate(POOL_WINDOWS):
                cols = slice(gi * gd, (gi + 1) * gd)
                win = ext[pl.ds(i0, rc + hist), rsl, cols]
                zt = win[hist:hist + rc]
                s = zt
                for d in range(1, w):
                    s = s + win[hist - d:hist - d + rc]
                inv = 1.0 / jnp.minimum(jnp.float32(w), posf + 1.0)
                pbuf[pl.ds(i0, rc), rsl, cols] = s * inv - zt
        return carry_

    lax.fori_loop(0, n_cur // rc, chunk, 0)

    pooled = pbuf[...].reshape(rows, c)
    outs_g = [_bdot(pooled[:, g * gd:(g + 1) * gd], w_grp[g]) for g in range(len(POOL_WINDOWS))]
    out = jnp.concatenate(outs_g, axis=-1) * scale[...]
    _store_rows(y_ref, x + _rms(out, g_ref[1:2, :]), sample, False)
    _store_state(st_ref, ext, hist, carry, t, nt, sample)


def _pool(x, prefix, j, pos0, gains, i, w_grp, scale, nb, length, rc):
    b, s, c = x.shape
    sample = prefix is not None
    assert (s == length) if sample else (nb == 1)
    n_cur, r8 = _time_tiles(sample, nb, length)
    in_specs = [_x_specs(sample, nb, length, c)]
    args = [x]
    if sample:
        in_specs.append(pl.BlockSpec((None, POOL_PREFIX, nb, c), lambda bb, tt: (j, 0, bb, 0)))
        args.append(prefix)
    in_specs += [
        pl.BlockSpec((None, N_NORMS, c), lambda bb, tt: (i, 0, 0)),
        pl.BlockSpec((None, len(POOL_WINDOWS), POOL_GROUP_DIM, POOL_GROUP_DIM), lambda bb, tt: (j, 0, 0, 0)),
        pl.BlockSpec((None, 1, c), lambda bb, tt: (j, 0, 0)),
    ]
    args += [gains, w_grp, scale]
    scratch = [pltpu.VMEM((POOL_PREFIX + n_cur, r8, c), F32), pltpu.VMEM((n_cur, r8, c), F32)]
    if not sample:
        scratch.append(pltpu.VMEM((POOL_PREFIX, SUBLANES, c), F32))
    return pl.pallas_call(
        functools.partial(_pool_kernel, sample, rc, pos0),
        grid=(b // nb, s // length),
        in_specs=in_specs,
        out_specs=[_x_specs(sample, nb, length, c), _state_spec(sample, nb, POOL_PREFIX, c, True)],
        out_shape=[jax.ShapeDtypeStruct((b, s, c), F32), _state_shape(sample, b, POOL_PREFIX, c, True)],
        scratch_shapes=scratch,
        compiler_params=_params(),
        name="pool_mixer",
    )(*args)


def _attend(qh, kh, vh):
    s = lax.dot_general(qh, kh.astype(BF16), (((1,), (1,)), ((), ())), preferred_element_type=F32)
    e = jnp.exp(s - jnp.max(s, axis=-1, keepdims=True))
    p = e / jnp.sum(e, axis=-1, keepdims=True)
    return jnp.dot(p.astype(BF16), vh.astype(BF16), preferred_element_type=F32)


def _attn_prompt_kernel(x_ref, k_ref, v_ref, g_ref, w_q, w_o, y_ref, obuf):
    hd = MEM_HEAD_DIM
    x = x_ref[...]
    q = (_bdot(_rms(x, g_ref[2:3, :]), w_q[...]) * (hd ** -0.5)).astype(BF16)
    for hh in range(N_MEM_HEADS):
        cols = slice(hh * hd, (hh + 1) * hd)
        obuf[:, cols] = _attend(q[:, cols], k_ref[:, cols], v_ref[:, cols])
    out = _bdot(obuf[...], w_o[...])
    y_ref[...] = x + _rms(out, g_ref[3:4, :])


def _cache_head(ref, n, hh):
    lane_tiles = MEM_HEAD_DIM // LANES
    stride = lane_tiles * N_MEM_HEADS
    parts = [ref[n, pl.ds(dt * N_MEM_HEADS + hh, N_MEM, stride=stride), :] for dt in range(lane_tiles)]
    return jnp.concatenate(parts, axis=-1)


def _attn_sample_kernel(x_ref, k_ref, v_ref, g_ref, w_q, w_o, y_ref, obuf):
    nb, length, c = x_ref.shape
    hd = MEM_HEAD_DIM
    nh = N_MEM_HEADS
    x = x_ref[...].reshape(nb * length, c)
    q = _bdot(_rms(x, g_ref[2:3, :]), w_q[...]) * (hd ** -0.5)
    col_head = lax.broadcasted_iota(jnp.int32, q.shape, 1) // hd
    q_heads = [jnp.where(col_head == hh, q, 0.0).astype(BF16) for hh in range(nh)]
    scores = []
    for n in range(nb):
        rsl = slice(n * length, (n + 1) * length)
        qn = jnp.concatenate([qh[rsl] for qh in q_heads], axis=0)
        kn = jnp.concatenate([_cache_head(k_ref, n, hh) for hh in range(nh)], axis=-1).astype(BF16)
        scores.append(lax.dot_general(qn, kn, (((1,), (1,)), ((), ())), preferred_element_type=F32))
    s = jnp.concatenate(scores, axis=0)
    e = jnp.exp(s - jnp.max(s, axis=-1, keepdims=True))
    p = (e / jnp.sum(e, axis=-1, keepdims=True)).astype(BF16)
    for n in range(nb):
        vn = jnp.concatenate([_cache_head(v_ref, n, hh) for hh in range(nh)], axis=-1).astype(BF16)
        o = jnp.dot(p[n * nh * length:(n + 1) * nh * length], vn, preferred_element_type=F32)
        for hh in range(nh):
            obuf[n * length:(n + 1) * length, hh * hd:(hh + 1) * hd] = (
                o[hh * length:(hh + 1) * length, hh * hd:(hh + 1) * hd])
    out = _bdot(obuf[...], w_o[...])
    y_ref[...] = (x + _rms(out, g_ref[3:4, :])).reshape(nb, length, c)


def _attn_prompt(x, mem_k, mem_v, gains, i, w_q, w_o, length):
    b, s, c = x.shape
    lay = lambda bb, tt: (i, 0, 0)
    return pl.pallas_call(
        _attn_prompt_kernel,
        grid=(b, s // length),
        in_specs=[
            pl.BlockSpec((None, length, c), lambda bb, tt: (bb, tt, 0)),
            pl.BlockSpec((None, None, N_MEM, c), lambda bb, tt: (i, bb, 0, 0)),
            pl.BlockSpec((None, None, N_MEM, c), lambda bb, tt: (i, bb, 0, 0)),
            pl.BlockSpec((None, N_NORMS, c), lay),
            pl.BlockSpec((None, c, c), lay, pipeline_mode=pl.Buffered(1)),
            pl.BlockSpec((None, c, c), lay, pipeline_mode=pl.Buffered(1)),
        ],
        out_specs=pl.BlockSpec((None, length, c), lambda bb, tt: (bb, tt, 0)),
        out_shape=jax.ShapeDtypeStruct((b, s, c), F32),
        scratch_shapes=[pltpu.VMEM((length, c), F32)],
        compiler_params=_params(),
        name="mem_cross_attn_prompt",
    )(x, mem_k, mem_v, gains, w_q, w_o)


def _attn_sample(x, cache_k, cache_v, gains, i, w_q, w_o, nb):
    b, length, c = x.shape
    lay = lambda bb: (i, 0, 0)
    kv_spec = pl.BlockSpec((None, nb) + cache_k.shape[2:], lambda bb: (i, bb, 0, 0))
    return pl.pallas_call(
        _attn_sample_kernel,
        grid=(b // nb,),
        in_specs=[
            pl.BlockSpec((nb, length, c), lambda bb: (bb, 0, 0)),
            kv_spec,
            kv_spec,
            pl.BlockSpec((None, N_NORMS, c), lay),
            pl.BlockSpec((None, c, c), lay, pipeline_mode=pl.Buffered(1)),
            pl.BlockSpec((None, c, c), lay, pipeline_mode=pl.Buffered(1)),
        ],
        out_specs=pl.BlockSpec((nb, length, c), lambda bb: (bb, 0, 0)),
        out_shape=jax.ShapeDtypeStruct((b, length, c), F32),
        scratch_shapes=[pltpu.VMEM((nb * length, c), F32)],
        compiler_params=_params(("arbitrary",)),
        name="mem_cross_attn_sample",
    )(x, cache_k, cache_v, gains, w_q, w_o)


def _ffn_kernel(sample, permute_out, *refs):
    x_ref, pre_ref, ins, outs, scr, carry = _split_refs(refs, sample, 5, 2, 0)
    g_ref, w_up, w_dw, b_dw, w_down = ins
    y_ref, st_ref = outs
    t = pl.program_id(1)
    nt = pl.num_programs(1)
    if sample:
        r8, n_cur = x_ref.shape[:2]
    else:
        n_cur, r8 = x_ref.shape[0] // SUBLANES, SUBLANES
    rows = n_cur * r8
    c = x_ref.shape[-1]
    cw = FFN_CHUNK
    hist = FFN_PREFIX

    x = _load_rows(x_ref, sample, False)
    hb = _rms(x, g_ref[4:5, :]).astype(BF16)

    if not sample:
        @pl.when(t == 0)
        def _():
            carry[...] = jnp.zeros(carry.shape, F32)

    def conv(cols):
        u = jnp.dot(hb, w_up[:, cols], preferred_element_type=F32)
        u3 = u.reshape(n_cur, r8, cw)
        if sample:
            past = jnp.stack([pre_ref[:, j, cols] for j in range(hist)])
            for j in range(hist):
                st_ref[:, j, cols] = u3[n_cur - hist + j]
        else:
            last = u3[n_cur - hist:]
            past = _segment_history(last, carry[:, :, cols])
            carry[:, :, cols] = last
        e = jnp.concatenate([past.reshape(hist * r8, cw), u], axis=0)
        y = b_dw[:, cols] + w_dw[FFN_CONV_WIDTH - 1:FFN_CONV_WIDTH, cols] * u
        for k in range(hist):
            y = y + w_dw[k:k + 1, cols] * e[k * r8:k * r8 + rows]
        return y

    acc = jnp.zeros((rows, c), F32)
    for ci in range(D_FF // cw):
        gcols = slice(ci * cw, (ci + 1) * cw)
        vcols = slice(D_FF + ci * cw, D_FF + (ci + 1) * cw)
        cg = conv(gcols)
        cv = conv(vcols)
        acc = acc + _bdot(cg * _sigmoid(cg) * cv, w_down[gcols, :])
    _store_rows(y_ref, x + _rms(acc, g_ref[5:6, :]), sample, permute_out)

    if not sample:
        @pl.when(t == nt - 1)
        def _():
            st_ref[...] = carry[:, SUBLANES - 1, :]


def _ffn(x, prefix, gains, i, w_up, w_dw, b_dw, w_down, nb, length, permute_out):
    b, s, c = x.shape
    f2 = 2 * D_FF
    sample = prefix is not None
    assert (s == length) if sample else (nb == 1)
    lay = lambda bb, tt: (i, 0, 0)
    single = pl.Buffered(1)
    in_specs = [_x_specs(sample, nb, length, c)]
    args = [x]
    if sample:
        in_specs.append(pl.BlockSpec((None, nb, FFN_PREFIX, f2), lambda bb, tt: (i, bb, 0, 0)))
        args.append(prefix)
    in_specs += [
        pl.BlockSpec((None, N_NORMS, c), lay),
        pl.BlockSpec((None, c, f2), lay, pipeline_mode=single),
        pl.BlockSpec((None, FFN_CONV_WIDTH, f2), lay),
        pl.BlockSpec((None, 1, f2), lay),
        pl.BlockSpec((None, D_FF, c), lay, pipeline_mode=single),
    ]
    args += [gains, w_up, w_dw, b_dw, w_down]
    scratch = [] if sample else [pltpu.VMEM((FFN_PREFIX, SUBLANES, f2), F32)]
    return pl.pallas_call(
        functools.partial(_ffn_kernel, sample, permute_out),
        grid=(b // nb, s // length),
        in_specs=in_specs,
        out_specs=[_x_specs(sample, nb, length, c), _state_spec(sample, nb, FFN_PREFIX, f2)],
        out_shape=[jax.ShapeDtypeStruct((b, s, c), F32), jax.ShapeDtypeStruct((b, FFN_PREFIX, f2), F32)],
        scratch_shapes=scratch,
        compiler_params=_params(),
        name="conv_ffn",
    )(*args)


def _mem_kv_kernel(m_ref, g_ref, w_kv, k_ref, v_ref):
    c = m_ref.shape[-1]
    kv = _bdot(_rms(m_ref[...], g_ref[N_NORMS - 1:N_NORMS, :]), w_kv[...])
    k_ref[...] = kv[:, :c]
    v_ref[...] = kv[:, c:]


def _mem_kv(mem, gains, w_kv, tile):
    r, c = mem.shape
    return pl.pallas_call(
        _mem_kv_kernel,
        grid=(DEPTH, r // tile),
        in_specs=[
            pl.BlockSpec((tile, c), lambda ii, rr: (rr, 0)),
            pl.BlockSpec((None, N_NORMS, c), lambda ii, rr: (ii, 0, 0)),
            pl.BlockSpec((None, c, 2 * c), lambda ii, rr: (ii, 0, 0)),
        ],
        out_specs=[
            pl.BlockSpec((None, tile, c), lambda ii, rr: (ii, rr, 0)),
            pl.BlockSpec((None, tile, c), lambda ii, rr: (ii, rr, 0)),
        ],
        out_shape=[
            jax.ShapeDtypeStruct((DEPTH, r, c), F32),
            jax.ShapeDtypeStruct((DEPTH, r, c), F32),
        ],
        compiler_params=_params(),
        name="mem_kv",
    )(mem, gains, w_kv)


_PLAN = {
    "prompt": dict(nb=1, length=PROMPT_TILE, conf_rc=8, pool_rc=8, attn_nb=1),
    "sample": dict(nb=32, length=8, conf_rc=8, pool_rc=8, attn_nb=8),
}


def _trunk(x, sample, pos0, conv_pre, pool_pre, ffn_pre, mem_k, mem_v, p):
    plan = _PLAN["sample" if sample else "prompt"]
    nb, length = plan["nb"], plan["length"]
    conv_states, pool_states, ffn_states = [], [], []
    for i in range(DEPTH):
        j = i // 2
        if i % 2 == 0:
            x, st = _conformer(x, conv_pre, j, p["gains"], i, p["a_w_in"], p["a_b_in"], p["a_w_dw"], p["a_b_dw"],
                               p["a_ln_g"], p["a_ln_b"], p["a_w_out"], p["a_b_out"], nb, length, plan["conf_rc"],
                               permute_in=(i == 0))
            conv_states.append(st)
        else:
            x, st = _pool(x, pool_pre, j, pos0, p["gains"], i, p["p_w_group"], p["p_scale"], nb, length,
                          plan["pool_rc"])
            pool_states.append(st)
        if sample:
            x = _attn_sample(x, mem_k, mem_v, p["gains"], i, p["c_w_q"], p["c_w_o"], plan["attn_nb"])
        else:
            x = _attn_prompt(x, mem_k, mem_v, p["gains"], i, p["c_w_q"], p["c_w_o"], length)
        x, st = _ffn(x, ffn_pre, p["gains"], i, p["f_w_up"], p["f_w_dw"], p["f_b_dw"], p["f_w_down"], nb, length,
                     permute_out=(i == DEPTH - 1))
        ffn_states.append(st)
    return x, jnp.stack(conv_states), jnp.stack(pool_states), jnp.stack(ffn_states)


def _cache_rows(cache):
    d, b = cache.shape[:2]
    lane_tiles = MEM_HEAD_DIM // LANES
    tiled = cache.reshape(d, b, N_MEM, N_MEM_HEADS, lane_tiles, LANES).transpose(0, 1, 2, 4, 3, 5)
    return tiled.reshape(d, b, N_MEM * lane_tiles * N_MEM_HEADS, LANES)


def kernel(x_prompt, x_sample, mem_prompt, state_conv, state_pool, state_ffn, cache_mem_k, cache_mem_v, norm_gains, a_w_in, a_b_in, a_w_dw, a_b_dw, a_ln_g, a_ln_b, a_w_out, a_b_out, p_w_group, p_scale, c_w_q, c_w_kv, c_w_o, f_w_up, f_w_dw, f_b_dw, f_w_down):
    b = x_prompt.shape[0]
    c = D_MODEL
    p = dict(
        gains=norm_gains,
        a_w_in=a_w_in.astype(BF16), a_b_in=a_b_in[:, None, :], a_w_dw=a_w_dw, a_b_dw=a_b_dw[:, None, :],
        a_ln_g=a_ln_g[:, None, :], a_ln_b=a_ln_b[:, None, :], a_w_out=a_w_out.astype(BF16),
        a_b_out=a_b_out[:, None, :],
        p_w_group=p_w_group.astype(BF16), p_scale=p_scale[:, None, :],
        c_w_q=c_w_q.astype(BF16), c_w_o=c_w_o.astype(BF16),
        f_w_up=f_w_up.astype(BF16), f_w_dw=f_w_dw, f_b_dw=f_b_dw[:, None, :], f_w_down=f_w_down.astype(BF16),
    )
    mk, mv = _mem_kv(mem_prompt.reshape(b * N_MEM, c), norm_gains, c_w_kv.astype(BF16), 512)
    mk = mk.reshape(DEPTH, b, N_MEM, c)
    mv = mv.reshape(DEPTH, b, N_MEM, c)
    y_p, conv_p, pool_p, ffn_p = _trunk(x_prompt, False, 0, None, None, None, mk, mv, p)
    tm = lambda a: a.transpose(0, 2, 1, 3)
    y_s, conv_s, pool_s, ffn_s = _trunk(x_sample, True, PAST_LEN, tm(state_conv), tm(state_pool), state_ffn,
                                        _cache_rows(cache_mem_k), _cache_rows(cache_mem_v), p)
    conv_s, pool_s = tm(conv_s), tm(pool_s)
    kv_shape = (DEPTH, b, N_MEM, N_MEM_HEADS, MEM_HEAD_DIM)
    return (y_p, y_s, conv_p, pool_p, ffn_p, mk.reshape(kv_shape), mv.reshape(kv_shape), conv_s, pool_s, ffn_s)
```

```python
import functools

import jax
import jax.numpy as jnp
from jax import lax
from jax.experimental import pallas as pl
from jax.experimental.pallas import tpu as pltpu

D_MODEL = 1024
DEPTH = 4
PAST_LEN = 16384
CONV_WIDTH = 31
CONV_PREFIX = CONV_WIDTH - 1
POOL_WINDOWS = (2, 4, 8, 16)
POOL_GROUP_DIM = D_MODEL // len(POOL_WINDOWS)
POOL_PREFIX = max(POOL_WINDOWS) - 1
N_MEM = 256
N_MEM_HEADS = 4
MEM_HEAD_DIM = D_MODEL // N_MEM_HEADS
D_FF = 2816
FFN_CONV_WIDTH = 3
FFN_PREFIX = FFN_CONV_WIDTH - 1
N_NORMS = 7
RMS_EPS = 1e-6
LN_EPS = 1e-5

SUBLANES = 8
LANES = 128
PROMPT_TILE = 512
PROMPT_SEG = PROMPT_TILE // SUBLANES
FFN_CHUNK = 256
VMEM_LIMIT_BYTES = 56 * 1024 * 1024

BF16 = jnp.bfloat16
F32 = jnp.float32


def _rms(x, g):
    ms = jnp.mean(x * x, axis=-1, keepdims=True)
    return x * lax.rsqrt(ms + RMS_EPS) * g


def _bdot(a, w):
    return jnp.dot(a.astype(BF16), w, preferred_element_type=F32)


def _sigmoid(x):
    return 1.0 / (1.0 + jnp.exp(-x))


def _to_segments(x):
    rows, c = x.shape
    return x.reshape(SUBLANES, rows // SUBLANES, c).swapaxes(0, 1).reshape(rows, c)


def _from_segments(x):
    rows, c = x.shape
    return x.reshape(rows // SUBLANES, SUBLANES, c).swapaxes(0, 1).reshape(rows, c)


def _load_rows(x_ref, sample, permute_in):
    if sample:
        return jnp.concatenate([x_ref[:, t, :] for t in range(x_ref.shape[1])], axis=0)
    x = x_ref[...]
    return _to_segments(x) if permute_in else x


def _store_rows(y_ref, y, sample, permute_out):
    if sample:
        nb = y_ref.shape[0]
        for t in range(y_ref.shape[1]):
            y_ref[:, t, :] = y[t * nb:(t + 1) * nb]
    else:
        y_ref[...] = _from_segments(y) if permute_out else y


def _segment_history(last, carry_val):
    sub = lax.broadcasted_iota(jnp.int32, last.shape, 1)
    return pltpu.roll(jnp.where(sub == SUBLANES - 1, carry_val, last), 1, axis=1)


def _fill_history(ext, cur3, hist, carry, pre_ref, t, sample):
    n_cur = cur3.shape[0]
    if sample:
        ext[0:hist] = pre_ref[...]
    else:
        @pl.when(t == 0)
        def _():
            carry[...] = jnp.zeros(carry.shape, F32)

        last = cur3[n_cur - hist:]
        ext[0:hist] = _segment_history(last, carry[...])
        carry[...] = last
    ext[hist:hist + n_cur] = cur3


def _store_state(st_ref, ext, hist, carry, t, nt, sample):
    n_cur = ext.shape[0] - hist
    if sample:
        st_ref[...] = ext[n_cur:n_cur + hist]
    else:
        @pl.when(t == nt - 1)
        def _():
            st_ref[...] = carry[:, SUBLANES - 1, :]


def _split_refs(refs, sample, n_in, n_out, n_scratch):
    refs = list(refs)
    x_ref = refs.pop(0)
    pre_ref = refs.pop(0) if sample else None
    ins, refs = refs[:n_in], refs[n_in:]
    outs, refs = refs[:n_out], refs[n_out:]
    scr, refs = refs[:n_scratch], refs[n_scratch:]
    carry = None if sample else refs.pop(0)
    assert not refs
    return x_ref, pre_ref, ins, outs, scr, carry


def _conformer_kernel(sample, permute_in, rc, *refs):
    x_ref, pre_ref, ins, outs, scr, carry = _split_refs(refs, sample, 9, 2, 2)
    g_ref, w_in, b_in, w_dw, b_dw, ln_g, ln_b, w_out, b_out = ins
    y_ref, st_ref = outs
    ext, cbuf = scr
    t = pl.program_id(1)
    nt = pl.num_programs(1)
    n_cur, r8, c = cbuf.shape
    rows = n_cur * r8

    x = _load_rows(x_ref, sample, permute_in)
    h = _rms(x, g_ref[0:1, :])
    u = _bdot(h, w_in[...]) + b_in[...]
    glu = u[:, :c] * _sigmoid(u[:, c:])
    _fill_history(ext, glu.reshape(n_cur, r8, c), CONV_PREFIX, carry, pre_ref, t, sample)

    def chunk(ci, carry_):
        i0 = ci * rc
        for g in range(r8 // SUBLANES):
            rsl = slice(g * SUBLANES, (g + 1) * SUBLANES)
            def lane_col(l, carry2):
                cols = pl.ds(pl.multiple_of(l * LANES, LANES), LANES)
                bias = jnp.broadcast_to(b_dw[:, cols], (SUBLANES, LANES))
                accs = [bias] * rc
                for m in range(rc + CONV_PREFIX):
                    tile = ext[i0 + m, rsl, cols]
                    for j in range(max(0, m - CONV_PREFIX), min(rc, m + 1)):
                        accs[j] = accs[j] + w_dw[m - j:m - j + 1, cols] * tile
                cbuf[pl.ds(i0, rc), rsl, cols] = jnp.stack(accs)
                return carry2

            lax.fori_loop(0, c // LANES, lane_col, 0)
        return carry_

    lax.fori_loop(0, n_cur // rc, chunk, 0)

    v = cbuf[...].reshape(rows, c)
    mu = jnp.mean(v, axis=-1, keepdims=True)
    xc = v - mu
    var = jnp.mean(xc * xc, axis=-1, keepdims=True)
    yn = xc * lax.rsqrt(var + LN_EPS) * ln_g[...] + ln_b[...]
    out = _bdot(yn * _sigmoid(yn), w_out[...]) + b_out[...]
    _store_rows(y_ref, x + _rms(out, g_ref[1:2, :]), sample, False)
    _store_state(st_ref, ext, CONV_PREFIX, carry, t, nt, sample)


def _x_specs(sample, nb, length, c):
    if sample:
        return pl.BlockSpec((nb, length, c), lambda bb, tt: (bb, 0, 0))
    return pl.BlockSpec((None, length, c), lambda bb, tt: (bb, tt, 0))


def _state_spec(sample, nb, hist, width, time_major=False):
    if sample and time_major:
        return pl.BlockSpec((hist, nb, width), lambda bb, tt: (0, bb, 0))
    if sample:
        return pl.BlockSpec((nb, hist, width), lambda bb, tt: (bb, 0, 0))
    return pl.BlockSpec((None, hist, width), lambda bb, tt: (bb, 0, 0))


def _state_shape(sample, b, hist, width, time_major=False):
    return jax.ShapeDtypeStruct((hist, b, width) if (sample and time_major) else (b, hist, width), F32)


def _time_tiles(sample, nb, length):
    return (length, nb) if sample else (length // SUBLANES, SUBLANES)


def _params(sem=("arbitrary", "arbitrary")):
    return pltpu.CompilerParams(dimension_semantics=sem, vmem_limit_bytes=VMEM_LIMIT_BYTES)


def _conformer(x, prefix, j, gains, i, w_in, b_in, w_dw, b_dw, ln_g, ln_b, w_out, b_out, nb, length, rc,
               permute_in):
    b, s, c = x.shape
    sample = prefix is not None
    assert (s == length) if sample else (nb == 1)
    n_cur, r8 = _time_tiles(sample, nb, length)
    lay = lambda bb, tt: (i, 0, 0)
    layj = lambda bb, tt: (j, 0, 0)
    single = pl.Buffered(1)
    in_specs = [_x_specs(sample, nb, length, c)]
    args = [x]
    if sample:
        in_specs.append(pl.BlockSpec((None, CONV_PREFIX, nb, c), lambda bb, tt: (j, 0, bb, 0)))
        args.append(prefix)
    in_specs += [
        pl.BlockSpec((None, N_NORMS, c), lay),
        pl.BlockSpec((None, c, 2 * c), layj, pipeline_mode=single),
        pl.BlockSpec((None, 1, 2 * c), layj),
        pl.BlockSpec((None, CONV_WIDTH, c), layj),
        pl.BlockSpec((None, 1, c), layj),
        pl.BlockSpec((None, 1, c), layj),
        pl.BlockSpec((None, 1, c), layj),
        pl.BlockSpec((None, c, c), layj, pipeline_mode=single),
        pl.BlockSpec((None, 1, c), layj),
    ]
    args += [gains, w_in, b_in, w_dw, b_dw, ln_g, ln_b, w_out, b_out]
    scratch = [pltpu.VMEM((CONV_PREFIX + n_cur, r8, c), F32), pltpu.VMEM((n_cur, r8, c), F32)]
    if not sample:
        scratch.append(pltpu.VMEM((CONV_PREFIX, SUBLANES, c), F32))
    return pl.pallas_call(
        functools.partial(_conformer_kernel, sample, permute_in, rc),
        grid=(b // nb, s // length),
        in_specs=in_specs,
        out_specs=[_x_specs(sample, nb, length, c), _state_spec(sample, nb, CONV_PREFIX, c, True)],
        out_shape=[jax.ShapeDtypeStruct((b, s, c), F32), _state_shape(sample, b, CONV_PREFIX, c, True)],
        scratch_shapes=scratch,
        compiler_params=_params(),
        name="conformer_mixer",
    )(*args)


def _pool_kernel(sample, rc, pos0, *refs):
    x_ref, pre_ref, ins, outs, scr, carry = _split_refs(refs, sample, 3, 2, 2)
    g_ref, w_grp, scale = ins
    y_ref, st_ref = outs
    ext, pbuf = scr
    t = pl.program_id(1)
    nt = pl.num_programs(1)
    n_cur, r8, c = pbuf.shape
    rows = n_cur * r8
    gd = POOL_GROUP_DIM
    hist = POOL_PREFIX

    x = _load_rows(x_ref, sample, False)
    h = _rms(x, g_ref[0:1, :])
    _fill_history(ext, h.reshape(n_cur, r8, c), hist, carry, pre_ref, t, sample)

    def chunk(ci, carry_):
        i0 = ci * rc
        step = i0 + lax.broadcasted_iota(jnp.int32, (rc, SUBLANES, 1), 0)
        if sample:
            pos = pos0 + step
        else:
            seg = lax.broadcasted_iota(jnp.int32, (rc, SUBLANES, 1), 1)
            pos = pos0 + t * rows + seg * n_cur + step
        posf = pos.astype(F32)
        for g in range(r8 // SUBLANES):
            rsl = slice(g * SUBLANES, (g + 1) * SUBLANES)
            for gi, w in enumerate(POOL_WINDOWS):
                cols = slice(gi * gd, (gi + 1) * gd)
                win = ext[pl.ds(i0, rc + hist), rsl, cols]
                zt = win[hist:hist + rc]
                s = zt
                for d in range(1, w):
                    s = s + win[hist - d:hist - d + rc]
                inv = 1.0 / jnp.minimum(jnp.float32(w), posf + 1.0)
                pbuf[pl.ds(i0, rc), rsl, cols] = s * inv - zt
        return carry_

    lax.fori_loop(0, n_cur // rc, chunk, 0)

    pooled = pbuf[...].reshape(rows, c)
    outs_g = [_bdot(pooled[:, g * gd:(g + 1) * gd], w_grp[g]) for g in range(len(POOL_WINDOWS))]
    out = jnp.concatenate(outs_g, axis=-1) * scale[...]
    _store_rows(y_ref, x + _rms(out, g_ref[1:2, :]), sample, False)
    _store_state(st_ref, ext, hist, carry, t, nt, sample)


def _pool(x, prefix, j, pos0, gains, i, w_grp, scale, nb, length, rc):
    b, s, c = x.shape
    sample = prefix is not None
    assert (s == length) if sample else (nb == 1)
    n_cur, r8 = _time_tiles(sample, nb, length)
    in_specs = [_x_specs(sample, nb, length, c)]
    args = [x]
    if sample:
        in_specs.append(pl.BlockSpec((None, POOL_PREFIX, nb, c), lambda bb, tt: (j, 0, bb, 0)))
        args.append(prefix)
    in_specs += [
        pl.BlockSpec((None, N_NORMS, c), lambda bb, tt: (i, 0, 0)),
        pl.BlockSpec((None, len(POOL_WINDOWS), POOL_GROUP_DIM, POOL_GROUP_DIM), lambda bb, tt: (j, 0, 0, 0)),
        pl.BlockSpec((None, 1, c), lambda bb, tt: (j, 0, 0)),
    ]
    args += [gains, w_grp, scale]
    scratch = [pltpu.VMEM((POOL_PREFIX + n_cur, r8, c), F32), pltpu.VMEM((n_cur, r8, c), F32)]
    if not sample:
        scratch.append(pltpu.VMEM((POOL_PREFIX, SUBLANES, c), F32))
    return pl.pallas_call(
        functools.partial(_pool_kernel, sample, rc, pos0),
        grid=(b // nb, s // length),
        in_specs=in_specs,
        out_specs=[_x_specs(sample, nb, length, c), _state_spec(sample, nb, POOL_PREFIX, c, True)],
        out_shape=[jax.ShapeDtypeStruct((b, s, c), F32), _state_shape(sample, b, POOL_PREFIX, c, True)],
        scratch_shapes=scratch,
        compiler_params=_params(),
        name="pool_mixer",
    )(*args)


def _attend(qh, kh, vh):
    s = lax.dot_general(qh, kh.astype(BF16), (((1,), (1,)), ((), ())), preferred_element_type=F32)
    e = jnp.exp(s - jnp.max(s, axis=-1, keepdims=True))
    p = e / jnp.sum(e, axis=-1, keepdims=True)
    return jnp.dot(p.astype(BF16), vh.astype(BF16), preferred_element_type=F32)


def _attn_prompt_kernel(x_ref, k_ref, v_ref, g_ref, w_q, w_o, y_ref, obuf):
    hd = MEM_HEAD_DIM
    x = x_ref[...]
    q = (_bdot(_rms(x, g_ref[2:3, :]), w_q[...]) * (hd ** -0.5)).astype(BF16)
    for hh in range(N_MEM_HEADS):
        cols = slice(hh * hd, (hh + 1) * hd)
        obuf[:, cols] = _attend(q[:, cols], _cache_head(k_ref, hh), _cache_head(v_ref, hh))
    out = _bdot(obuf[...], w_o[...])
    y_ref[...] = x + _rms(out, g_ref[3:4, :])


def _cache_head(ref, hh):
    lane_tiles = MEM_HEAD_DIM // LANES
    stride = lane_tiles * N_MEM_HEADS
    parts = [ref[pl.ds(dt * N_MEM_HEADS + hh, N_MEM, stride=stride), :] for dt in range(lane_tiles)]
    return jnp.concatenate(parts, axis=-1)


def _attn_sample_kernel(x_ref, k_ref, v_ref, g_ref, w_q, w_o, y_ref, obuf):
    nb, length, c = x_ref.shape
    hd = MEM_HEAD_DIM
    nh = N_MEM_HEADS
    x = x_ref[...].reshape(nb * length, c)
    q = _bdot(_rms(x, g_ref[2:3, :]), w_q[...]) * (hd ** -0.5)
    col_head = lax.broadcasted_iota(jnp.int32, q.shape, 1) // hd
    q_heads = [jnp.where(col_head == hh, q, 0.0).astype(BF16) for hh in range(nh)]
    scores = []
    for n in range(nb):
        rsl = slice(n * length, (n + 1) * length)
        qn = jnp.concatenate([qh[rsl] for qh in q_heads], axis=0)
        kn = jnp.concatenate([_cache_head(k_ref.at[n], hh) for hh in range(nh)], axis=-1).astype(BF16)
        scores.append(lax.dot_general(qn, kn, (((1,), (1,)), ((), ())), preferred_element_type=F32))
    s = jnp.concatenate(scores, axis=0)
    e = jnp.exp(s - jnp.max(s, axis=-1, keepdims=True))
    p = (e / jnp.sum(e, axis=-1, keepdims=True)).astype(BF16)
    for n in range(nb):
        vn = jnp.concatenate([_cache_head(v_ref.at[n], hh) for hh in range(nh)], axis=-1).astype(BF16)
        o = jnp.dot(p[n * nh * length:(n + 1) * nh * length], vn, preferred_element_type=F32)
        for hh in range(nh):
            obuf[n * length:(n + 1) * length, hh * hd:(hh + 1) * hd] = (
                o[hh * length:(hh + 1) * length, hh * hd:(hh + 1) * hd])
    out = _bdot(obuf[...], w_o[...])
    y_ref[...] = (x + _rms(out, g_ref[3:4, :])).reshape(nb, length, c)


def _attn_prompt(x, mem_k, mem_v, gains, i, w_q, w_o, length):
    b, s, c = x.shape
    lay = lambda bb, tt: (i, 0, 0)
    return pl.pallas_call(
        _attn_prompt_kernel,
        grid=(b, s // length),
        in_specs=[
            pl.BlockSpec((None, length, c), lambda bb, tt: (bb, tt, 0)),
            pl.BlockSpec((None, None) + mem_k.shape[2:], lambda bb, tt: (i, bb, 0, 0)),
            pl.BlockSpec((None, None) + mem_k.shape[2:], lambda bb, tt: (i, bb, 0, 0)),
            pl.BlockSpec((None, N_NORMS, c), lay),
            pl.BlockSpec((None, c, c), lay, pipeline_mode=pl.Buffered(1)),
            pl.BlockSpec((None, c, c), lay, pipeline_mode=pl.Buffered(1)),
        ],
        out_specs=pl.BlockSpec((None, length, c), lambda bb, tt: (bb, tt, 0)),
        out_shape=jax.ShapeDtypeStruct((b, s, c), F32),
        scratch_shapes=[pltpu.VMEM((length, c), F32)],
        compiler_params=_params(),
        name="mem_cross_attn_prompt",
    )(x, mem_k, mem_v, gains, w_q, w_o)


def _attn_sample(x, cache_k, cache_v, gains, i, w_q, w_o, nb):
    b, length, c = x.shape
    lay = lambda bb: (i, 0, 0)
    kv_spec = pl.BlockSpec((None, nb) + cache_k.shape[2:], lambda bb: (i, bb, 0, 0))
    return pl.pallas_call(
        _attn_sample_kernel,
        grid=(b // nb,),
        in_specs=[
            pl.BlockSpec((nb, length, c), lambda bb: (bb, 0, 0)),
            kv_spec,
            kv_spec,
            pl.BlockSpec((None, N_NORMS, c), lay),
            pl.BlockSpec((None, c, c), lay, pipeline_mode=pl.Buffered(1)),
            pl.BlockSpec((None, c, c), lay, pipeline_mode=pl.Buffered(1)),
        ],
        out_specs=pl.BlockSpec((nb, length, c), lambda bb: (bb, 0, 0)),
        out_shape=jax.ShapeDtypeStruct((b, length, c), F32),
        scratch_shapes=[pltpu.VMEM((nb * length, c), F32)],
        compiler_params=_params(("arbitrary",)),
        name="mem_cross_attn_sample",
    )(x, cache_k, cache_v, gains, w_q, w_o)


def _ffn_kernel(sample, permute_out, *refs):
    x_ref, pre_ref, ins, outs, (act,), carry = _split_refs(refs, sample, 5, 2, 1)
    g_ref, w_up, w_dw, b_dw, w_down = ins
    y_ref, st_ref = outs
    t = pl.program_id(1)
    nt = pl.num_programs(1)
    if sample:
        r8, n_cur = x_ref.shape[:2]
    else:
        n_cur, r8 = x_ref.shape[0] // SUBLANES, SUBLANES
    rows = n_cur * r8
    c = x_ref.shape[-1]
    cw = FFN_CHUNK
    hist = FFN_PREFIX

    x = _load_rows(x_ref, sample, False)
    hb = _rms(x, g_ref[4:5, :]).astype(BF16)

    if not sample:
        @pl.when(t == 0)
        def _():
            carry[...] = jnp.zeros(carry.shape, F32)

    def conv(cols):
        u = jnp.dot(hb, w_up[:, cols], preferred_element_type=F32)
        u3 = u.reshape(n_cur, r8, cw)
        if sample:
            past = jnp.stack([pre_ref[:, j, cols] for j in range(hist)])
            for j in range(hist):
                st_ref[:, j, cols] = u3[n_cur - hist + j]
        else:
            last = u3[n_cur - hist:]
            past = _segment_history(last, carry[:, :, cols])
            carry[:, :, cols] = last
        e = jnp.concatenate([past.reshape(hist * r8, cw), u], axis=0)
        y = b_dw[:, cols] + w_dw[FFN_CONV_WIDTH - 1:FFN_CONV_WIDTH, cols] * u
        for k in range(hist):
            y = y + w_dw[k:k + 1, cols] * e[k * r8:k * r8 + rows]
        return y

    for ci in range(D_FF // cw):
        gcols = slice(ci * cw, (ci + 1) * cw)
        vcols = slice(D_FF + ci * cw, D_FF + (ci + 1) * cw)
        cg = conv(gcols)
        cv = conv(vcols)
        act[:, gcols] = (cg * _sigmoid(cg) * cv).astype(BF16)
    out = jnp.dot(act[...], w_down[...], preferred_element_type=F32)
    _store_rows(y_ref, x + _rms(out, g_ref[5:6, :]), sample, permute_out)

    if not sample:
        @pl.when(t == nt - 1)
        def _():
            st_ref[...] = carry[:, SUBLANES - 1, :]


def _ffn(x, prefix, gains, i, w_up, w_dw, b_dw, w_down, nb, length, permute_out):
    b, s, c = x.shape
    f2 = 2 * D_FF
    sample = prefix is not None
    assert (s == length) if sample else (nb == 1)
    lay = lambda bb, tt: (i, 0, 0)
    single = pl.Buffered(1)
    in_specs = [_x_specs(sample, nb, length, c)]
    args = [x]
    if sample:
        in_specs.append(pl.BlockSpec((None, nb, FFN_PREFIX, f2), lambda bb, tt: (i, bb, 0, 0)))
        args.append(prefix)
    in_specs += [
        pl.BlockSpec((None, N_NORMS, c), lay),
        pl.BlockSpec((None, c, f2), lay, pipeline_mode=single),
        pl.BlockSpec((None, FFN_CONV_WIDTH, f2), lay),
        pl.BlockSpec((None, 1, f2), lay),
        pl.BlockSpec((None, D_FF, c), lay, pipeline_mode=single),
    ]
    args += [gains, w_up, w_dw, b_dw, w_down]
    scratch = [pltpu.VMEM((nb * length, D_FF), BF16)]
    if not sample:
        scratch.append(pltpu.VMEM((FFN_PREFIX, SUBLANES, f2), F32))
    return pl.pallas_call(
        functools.partial(_ffn_kernel, sample, permute_out),
        grid=(b // nb, s // length),
        in_specs=in_specs,
        out_specs=[_x_specs(sample, nb, length, c), _state_spec(sample, nb, FFN_PREFIX, f2)],
        out_shape=[jax.ShapeDtypeStruct((b, s, c), F32), jax.ShapeDtypeStruct((b, FFN_PREFIX, f2), F32)],
        scratch_shapes=scratch,
        compiler_params=_params(),
        name="conv_ffn",
    )(*args)


def _mem_kv_kernel(m_ref, g_ref, w_kv, k_ref, v_ref):
    nbm, n_mem, c = m_ref.shape
    kv = _bdot(_rms(m_ref[...].reshape(nbm * n_mem, c), g_ref[N_NORMS - 1:N_NORMS, :]), w_kv[...])
    lane_tiles = MEM_HEAD_DIM // LANES
    stride = lane_tiles * N_MEM_HEADS
    for ref, base in ((k_ref, 0), (v_ref, c)):
        for bi in range(nbm):
            for hh in range(N_MEM_HEADS):
                for dt in range(lane_tiles):
                    col = base + hh * MEM_HEAD_DIM + dt * LANES
                    ref[bi, pl.ds(dt * N_MEM_HEADS + hh, n_mem, stride=stride), :] = (
                        kv[bi * n_mem:(bi + 1) * n_mem, col:col + LANES])


def _mem_kv(mem, gains, w_kv, nbm):
    b, n_mem, c = mem.shape
    crows = n_mem * (c // LANES)
    out_spec = pl.BlockSpec((None, nbm, crows, LANES), lambda ii, rr: (ii, rr, 0, 0))
    out_shape = jax.ShapeDtypeStruct((DEPTH, b, crows, LANES), F32)
    return pl.pallas_call(
        _mem_kv_kernel,
        grid=(DEPTH, b // nbm),
        in_specs=[
            pl.BlockSpec((nbm, n_mem, c), lambda ii, rr: (rr, 0, 0)),
            pl.BlockSpec((None, N_NORMS, c), lambda ii, rr: (ii, 0, 0)),
            pl.BlockSpec((None, c, 2 * c), lambda ii, rr: (ii, 0, 0)),
        ],
        out_specs=[out_spec, out_spec],
        out_shape=[out_shape, out_shape],
        compiler_params=_params(),
        name="mem_kv",
    )(mem, gains, w_kv)


_PLAN = {
    "prompt": dict(nb=1, length=PROMPT_TILE, conf_rc=8, pool_rc=8, attn_length=2 * PROMPT_TILE),
    "sample": dict(nb=32, length=8, conf_rc=8, pool_rc=8, attn_nb=8),
}


def _trunk(x, sample, pos0, conv_pre, pool_pre, ffn_pre, mem_k, mem_v, p):
    plan = _PLAN["sample" if sample else "prompt"]
    nb, length = plan["nb"], plan["length"]
    conv_states, pool_states, ffn_states = [], [], []
    for i in range(DEPTH):
        j = i // 2
        if i % 2 == 0:
            x, st = _conformer(x, conv_pre, j, p["gains"], i, p["a_w_in"], p["a_b_in"], p["a_w_dw"], p["a_b_dw"],
                               p["a_ln_g"], p["a_ln_b"], p["a_w_out"], p["a_b_out"], nb, length, plan["conf_rc"],
                               permute_in=(i == 0))
            conv_states.append(st)
        else:
            x, st = _pool(x, pool_pre, j, pos0, p["gains"], i, p["p_w_group"], p["p_scale"], nb, length,
                          plan["pool_rc"])
            pool_states.append(st)
        if sample:
            x = _attn_sample(x, mem_k, mem_v, p["gains"], i, p["c_w_q"], p["c_w_o"], plan["attn_nb"])
        else:
            x = _attn_prompt(x, mem_k, mem_v, p["gains"], i, p["c_w_q"], p["c_w_o"], plan["attn_length"])
        x, st = _ffn(x, ffn_pre, p["gains"], i, p["f_w_up"], p["f_w_dw"], p["f_b_dw"], p["f_w_down"], nb, length,
                     permute_out=(i == DEPTH - 1))
        ffn_states.append(st)
    return x, jnp.stack(conv_states), jnp.stack(pool_states), jnp.stack(ffn_states)


def _cache_rows(cache):
    d, b = cache.shape[:2]
    lane_tiles = MEM_HEAD_DIM // LANES
    tiled = cache.reshape(d, b, N_MEM, N_MEM_HEADS, lane_tiles, LANES).transpose(0, 1, 2, 4, 3, 5)
    return tiled.reshape(d, b, N_MEM * lane_tiles * N_MEM_HEADS, LANES)


def _cache_unrows(rows):
    d, b = rows.shape[:2]
    lane_tiles = MEM_HEAD_DIM // LANES
    tiled = rows.reshape(d, b, N_MEM, lane_tiles, N_MEM_HEADS, LANES).transpose(0, 1, 2, 4, 3, 5)
    return tiled.reshape(d, b, N_MEM, N_MEM_HEADS, MEM_HEAD_DIM)


def kernel(x_prompt, x_sample, mem_prompt, state_conv, state_pool, state_ffn, cache_mem_k, cache_mem_v, norm_gains, a_w_in, a_b_in, a_w_dw, a_b_dw, a_ln_g, a_ln_b, a_w_out, a_b_out, p_w_group, p_scale, c_w_q, c_w_kv, c_w_o, f_w_up, f_w_dw, f_b_dw, f_w_down):
    b = x_prompt.shape[0]
    c = D_MODEL
    p = dict(
        gains=norm_gains,
        a_w_in=a_w_in.astype(BF16), a_b_in=a_b_in[:, None, :], a_w_dw=a_w_dw, a_b_dw=a_b_dw[:, None, :],
        a_ln_g=a_ln_g[:, None, :], a_ln_b=a_ln_b[:, None, :], a_w_out=a_w_out.astype(BF16),
        a_b_out=a_b_out[:, None, :],
        p_w_group=p_w_group.astype(BF16), p_scale=p_scale[:, None, :],
        c_w_q=c_w_q.astype(BF16), c_w_o=c_w_o.astype(BF16),
        f_w_up=f_w_up.astype(BF16), f_w_dw=f_w_dw, f_b_dw=f_b_dw[:, None, :], f_w_down=f_w_down.astype(BF16),
    )
    mk, mv = _mem_kv(mem_prompt, norm_gains, c_w_kv.astype(BF16), 2)
    y_p, conv_p, pool_p, ffn_p = _trunk(x_prompt, False, 0, None, None, None, mk, mv, p)
    tm = lambda a: a.transpose(0, 2, 1, 3)
    y_s, conv_s, pool_s, ffn_s = _trunk(x_sample, True, PAST_LEN, tm(state_conv), tm(state_pool), state_ffn,
                                        _cache_rows(cache_mem_k), _cache_rows(cache_mem_v), p)
    conv_s, pool_s = tm(conv_s), tm(pool_s)
    return (y_p, y_s, conv_p, pool_p, ffn_p, _cache_unrows(mk), _cache_unrows(mv), conv_s, pool_s, ffn_s)
```

```python
import functools

import jax
import jax.numpy as jnp
from jax import lax
from jax.experimental import pallas as pl
from jax.experimental.pallas import tpu as pltpu

D_MODEL = 1024
DEPTH = 4
PAST_LEN = 16384
CONV_WIDTH = 31
CONV_PREFIX = CONV_WIDTH - 1
POOL_WINDOWS = (2, 4, 8, 16)
POOL_GROUP_DIM = D_MODEL // len(POOL_WINDOWS)
POOL_PREFIX = max(POOL_WINDOWS) - 1
N_MEM = 256
N_MEM_HEADS = 4
MEM_HEAD_DIM = D_MODEL // N_MEM_HEADS
D_FF = 2816
FFN_CONV_WIDTH = 3
FFN_PREFIX = FFN_CONV_WIDTH - 1
N_NORMS = 7
RMS_EPS = 1e-6
LN_EPS = 1e-5

SUBLANES = 8
LANES = 128
PROMPT_TILE = 512
PROMPT_SEG = PROMPT_TILE // SUBLANES
FFN_CHUNK = 256
VMEM_LIMIT_BYTES = 56 * 1024 * 1024

BF16 = jnp.bfloat16
F32 = jnp.float32


def _rms(x, g):
    ms = jnp.mean(x * x, axis=-1, keepdims=True)
    return x * lax.rsqrt(ms + RMS_EPS) * g


def _bdot(a, w):
    return jnp.dot(a.astype(BF16), w, preferred_element_type=F32)


def _sigmoid(x):
    return 0.5 * jnp.tanh(0.5 * x) + 0.5


def _to_segments(x):
    rows, c = x.shape
    return x.reshape(SUBLANES, rows // SUBLANES, c).swapaxes(0, 1).reshape(rows, c)


def _from_segments(x):
    rows, c = x.shape
    return x.reshape(rows // SUBLANES, SUBLANES, c).swapaxes(0, 1).reshape(rows, c)


def _load_rows(x_ref, sample, permute_in):
    if sample:
        return jnp.concatenate([x_ref[:, t, :] for t in range(x_ref.shape[1])], axis=0)
    x = x_ref[...]
    return _to_segments(x) if permute_in else x


def _store_rows(y_ref, y, sample, permute_out):
    if sample:
        nb = y_ref.shape[0]
        for t in range(y_ref.shape[1]):
            y_ref[:, t, :] = y[t * nb:(t + 1) * nb]
    else:
        y_ref[...] = _from_segments(y) if permute_out else y


def _segment_history(last, carry_val):
    sub = lax.broadcasted_iota(jnp.int32, last.shape, 1)
    return pltpu.roll(jnp.where(sub == SUBLANES - 1, carry_val, last), 1, axis=1)


def _fill_history(ext, cur3, hist, carry, pre_ref, t, sample):
    n_cur = cur3.shape[0]
    if sample:
        ext[0:hist] = pre_ref[...]
    else:
        @pl.when(t == 0)
        def _():
            carry[...] = jnp.zeros(carry.shape, F32)

        last = cur3[n_cur - hist:]
        ext[0:hist] = _segment_history(last, carry[...])
        carry[...] = last
    ext[hist:hist + n_cur] = cur3


def _store_state(st_ref, ext, hist, carry, t, nt, sample):
    n_cur = ext.shape[0] - hist
    if sample:
        st_ref[...] = ext[n_cur:n_cur + hist]
    else:
        @pl.when(t == nt - 1)
        def _():
            st_ref[...] = carry[:, SUBLANES - 1, :]


def _split_refs(refs, sample, n_in, n_out, n_scratch):
    refs = list(refs)
    x_ref = refs.pop(0)
    pre_ref = refs.pop(0) if sample else None
    ins, refs = refs[:n_in], refs[n_in:]
    outs, refs = refs[:n_out], refs[n_out:]
    scr, refs = refs[:n_scratch], refs[n_scratch:]
    carry = None if sample else refs.pop(0)
    assert not refs
    return x_ref, pre_ref, ins, outs, scr, carry


def _conformer_kernel(sample, permute_in, rc, *refs):
    x_ref, pre_ref, ins, outs, scr, carry = _split_refs(refs, sample, 9, 2, 2)
    g_ref, w_in, b_in, w_dw, b_dw, ln_g, ln_b, w_out, b_out = ins
    y_ref, st_ref = outs
    ext, cbuf = scr
    t = pl.program_id(1)
    nt = pl.num_programs(1)
    n_cur, r8, c = cbuf.shape
    rows = n_cur * r8

    x = _load_rows(x_ref, sample, permute_in)
    h = _rms(x, g_ref[0:1, :])
    u = _bdot(h, w_in[...]) + b_in[...]
    glu = u[:, :c] * _sigmoid(u[:, c:])
    _fill_history(ext, glu.reshape(n_cur, r8, c), CONV_PREFIX, carry, pre_ref, t, sample)

    def chunk(ci, carry_):
        i0 = ci * rc
        for g in range(r8 // SUBLANES):
            rsl = slice(g * SUBLANES, (g + 1) * SUBLANES)
            def lane_col(l, carry2):
                cols = pl.ds(pl.multiple_of(l * LANES, LANES), LANES)
                bias = jnp.broadcast_to(b_dw[:, cols], (SUBLANES, LANES))
                accs = [bias] * rc
                for m in range(rc + CONV_PREFIX):
                    tile = ext[i0 + m, rsl, cols]
                    for j in range(max(0, m - CONV_PREFIX), min(rc, m + 1)):
                        accs[j] = accs[j] + w_dw[m - j:m - j + 1, cols] * tile
                cbuf[pl.ds(i0, rc), rsl, cols] = jnp.stack(accs)
                return carry2

            lax.fori_loop(0, c // LANES, lane_col, 0)
        return carry_

    lax.fori_loop(0, n_cur // rc, chunk, 0)

    v = cbuf[...].reshape(rows, c)
    mu = jnp.mean(v, axis=-1, keepdims=True)
    xc = v - mu
    var = jnp.mean(xc * xc, axis=-1, keepdims=True)
    yn = xc * lax.rsqrt(var + LN_EPS) * ln_g[...] + ln_b[...]
    out = _bdot(yn * _sigmoid(yn), w_out[...]) + b_out[...]
    _store_rows(y_ref, x + _rms(out, g_ref[1:2, :]), sample, False)
    _store_state(st_ref, ext, CONV_PREFIX, carry, t, nt, sample)


def _x_specs(sample, nb, length, c):
    if sample:
        return pl.BlockSpec((nb, length, c), lambda bb, tt: (bb, 0, 0))
    return pl.BlockSpec((None, length, c), lambda bb, tt: (bb, tt, 0))


def _state_spec(sample, nb, hist, width, time_major=False):
    if sample and time_major:
        return pl.BlockSpec((hist, nb, width), lambda bb, tt: (0, bb, 0))
    if sample:
        return pl.BlockSpec((nb, hist, width), lambda bb, tt: (bb, 0, 0))
    return pl.BlockSpec((None, hist, width), lambda bb, tt: (bb, 0, 0))


def _state_shape(sample, b, hist, width, time_major=False):
    return jax.ShapeDtypeStruct((hist, b, width) if (sample and time_major) else (b, hist, width), F32)


def _time_tiles(sample, nb, length):
    return (length, nb) if sample else (length // SUBLANES, SUBLANES)


def _params(sem=("arbitrary", "arbitrary")):
    return pltpu.CompilerParams(dimension_semantics=sem, vmem_limit_bytes=VMEM_LIMIT_BYTES)


def _conformer(x, prefix, j, gains, i, w_in, b_in, w_dw, b_dw, ln_g, ln_b, w_out, b_out, nb, length, rc,
               permute_in):
    b, s, c = x.shape
    sample = prefix is not None
    assert (s == length) if sample else (nb == 1)
    n_cur, r8 = _time_tiles(sample, nb, length)
    lay = lambda bb, tt: (i, 0, 0)
    layj = lambda bb, tt: (j, 0, 0)
    single = pl.Buffered(1)
    in_specs = [_x_specs(sample, nb, length, c)]
    args = [x]
    if sample:
        in_specs.append(pl.BlockSpec((None, CONV_PREFIX, nb, c), lambda bb, tt: (j, 0, bb, 0)))
        args.append(prefix)
    in_specs += [
        pl.BlockSpec((None, N_NORMS, c), lay),
        pl.BlockSpec((None, c, 2 * c), layj, pipeline_mode=single),
        pl.BlockSpec((None, 1, 2 * c), layj),
        pl.BlockSpec((None, CONV_WIDTH, c), layj),
        pl.BlockSpec((None, 1, c), layj),
        pl.BlockSpec((None, 1, c), layj),
        pl.BlockSpec((None, 1, c), layj),
        pl.BlockSpec((None, c, c), layj, pipeline_mode=single),
        pl.BlockSpec((None, 1, c), layj),
    ]
    args += [gains, w_in, b_in, w_dw, b_dw, ln_g, ln_b, w_out, b_out]
    scratch = [pltpu.VMEM((CONV_PREFIX + n_cur, r8, c), F32), pltpu.VMEM((n_cur, r8, c), F32)]
    if not sample:
        scratch.append(pltpu.VMEM((CONV_PREFIX, SUBLANES, c), F32))
    return pl.pallas_call(
        functools.partial(_conformer_kernel, sample, permute_in, rc),
        grid=(b // nb, s // length),
        in_specs=in_specs,
        out_specs=[_x_specs(sample, nb, length, c), _state_spec(sample, nb, CONV_PREFIX, c, True)],
        out_shape=[jax.ShapeDtypeStruct((b, s, c), F32), _state_shape(sample, b, CONV_PREFIX, c, True)],
        scratch_shapes=scratch,
        compiler_params=_params(),
        name="conformer_mixer",
    )(*args)


def _pool_kernel(sample, pos0, *refs):
    x_ref, pre_ref, ins, outs, _, carry = _split_refs(refs, sample, 3, 2, 0)
    g_ref, w_grp, scale = ins
    y_ref, st_ref = outs
    t = pl.program_id(1)
    nt = pl.num_programs(1)
    if sample:
        r8, n_cur = x_ref.shape[:2]
    else:
        n_cur, r8 = x_ref.shape[0] // SUBLANES, SUBLANES
    c = x_ref.shape[-1]
    rows = n_cur * r8
    gd = POOL_GROUP_DIM
    hist = POOL_PREFIX

    x = _load_rows(x_ref, sample, False)
    h3 = _rms(x, g_ref[0:1, :]).reshape(n_cur, r8, c)
    if sample:
        past = pre_ref[...]
    else:
        @pl.when(t == 0)
        def _():
            carry[...] = jnp.zeros(carry.shape, F32)

        last = h3[n_cur - hist:]
        past = _segment_history(last, carry[...])
        carry[...] = last
    e = jnp.concatenate([past, h3], axis=0)

    step = lax.broadcasted_iota(jnp.int32, (n_cur, r8, 1), 0)
    if sample:
        pos = pos0 + step
    else:
        pos = pos0 + t * rows + lax.broadcasted_iota(jnp.int32, (n_cur, r8, 1), 1) * n_cur + step
    posf = pos.astype(F32)

    outs_g = []
    for gi, w in enumerate(POOL_WINDOWS):
        eg = e[:, :, gi * gd:(gi + 1) * gd]
        s, m = eg, 1
        while m < w:
            s = s[m:] + s[:-m]
            m *= 2
        first = hist - (w - 1)
        inv = 1.0 / jnp.minimum(jnp.float32(w), posf + 1.0)
        pooled = s[first:first + n_cur] * inv - eg[hist:]
        outs_g.append(_bdot(pooled.reshape(rows, gd), w_grp[gi]))
    out = jnp.concatenate(outs_g, axis=-1) * scale[...]
    _store_rows(y_ref, x + _rms(out, g_ref[1:2, :]), sample, False)
    if sample:
        st_ref[...] = e[n_cur:]
    else:
        @pl.when(t == nt - 1)
        def _():
            st_ref[...] = carry[:, SUBLANES - 1, :]


def _pool(x, prefix, j, pos0, gains, i, w_grp, scale, nb, length):
    b, s, c = x.shape
    sample = prefix is not None
    assert (s == length) if sample else (nb == 1)
    in_specs = [_x_specs(sample, nb, length, c)]
    args = [x]
    if sample:
        in_specs.append(pl.BlockSpec((None, POOL_PREFIX, nb, c), lambda bb, tt: (j, 0, bb, 0)))
        args.append(prefix)
    in_specs += [
        pl.BlockSpec((None, N_NORMS, c), lambda bb, tt: (i, 0, 0)),
        pl.BlockSpec((None, len(POOL_WINDOWS), POOL_GROUP_DIM, POOL_GROUP_DIM), lambda bb, tt: (j, 0, 0, 0)),
        pl.BlockSpec((None, 1, c), lambda bb, tt: (j, 0, 0)),
    ]
    args += [gains, w_grp, scale]
    scratch = [] if sample else [pltpu.VMEM((POOL_PREFIX, SUBLANES, c), F32)]
    return pl.pallas_call(
        functools.partial(_pool_kernel, sample, pos0),
        grid=(b // nb, s // length),
        in_specs=in_specs,
        out_specs=[_x_specs(sample, nb, length, c), _state_spec(sample, nb, POOL_PREFIX, c, True)],
        out_shape=[jax.ShapeDtypeStruct((b, s, c), F32), _state_shape(sample, b, POOL_PREFIX, c, True)],
        scratch_shapes=scratch,
        compiler_params=_params(),
        name="pool_mixer",
    )(*args)


def _attend(qh, kh, vh):
    s = lax.dot_general(qh, kh.astype(BF16), (((1,), (1,)), ((), ())), preferred_element_type=F32)
    e = jnp.exp(s - jnp.max(s, axis=-1, keepdims=True))
    p = e * (1.0 / jnp.sum(e, axis=-1, keepdims=True))
    return jnp.dot(p.astype(BF16), vh.astype(BF16), preferred_element_type=F32)


def _cache_head(ref, hh):
    lane_tiles = MEM_HEAD_DIM // LANES
    stride = lane_tiles * N_MEM_HEADS
    parts = [ref[pl.ds(dt * N_MEM_HEADS + hh, N_MEM, stride=stride), :] for dt in range(lane_tiles)]
    return jnp.concatenate(parts, axis=-1)


def _attn_kernel(xp_ref, xs_ref, kp_ref, vp_ref, ks_ref, vs_ref, g_ref, w_q, w_o, yp_ref, ys_ref, obuf):
    rows_p, c = xp_ref.shape
    nb, length, _ = xs_ref.shape
    hd = MEM_HEAD_DIM
    nh = N_MEM_HEADS
    x = jnp.concatenate([xp_ref[...], xs_ref[...].reshape(nb * length, c)], axis=0)
    q = _bdot(_rms(x, g_ref[2:3, :]), w_q[...]) * (hd ** -0.5)

    qp = q[:rows_p].astype(BF16)
    for hh in range(nh):
        cols = slice(hh * hd, (hh + 1) * hd)
        obuf[0:rows_p, cols] = _attend(qp[:, cols], _cache_head(kp_ref, hh), _cache_head(vp_ref, hh))

    qs = q[rows_p:]
    col_head = lax.broadcasted_iota(jnp.int32, qs.shape, 1) // hd
    q_heads = [jnp.where(col_head == hh, qs, 0.0).astype(BF16) for hh in range(nh)]
    scores = []
    for n in range(nb):
        rsl = slice(n * length, (n + 1) * length)
        qn = jnp.concatenate([qh[rsl] for qh in q_heads], axis=0)
        kn = jnp.concatenate([_cache_head(ks_ref.at[n], hh) for hh in range(nh)], axis=-1).astype(BF16)
        scores.append(lax.dot_general(qn, kn, (((1,), (1,)), ((), ())), preferred_element_type=F32))
    s = jnp.concatenate(scores, axis=0)
    e = jnp.exp(s - jnp.max(s, axis=-1, keepdims=True))
    p = (e * (1.0 / jnp.sum(e, axis=-1, keepdims=True))).astype(BF16)
    for n in range(nb):
        vn = jnp.concatenate([_cache_head(vs_ref.at[n], hh) for hh in range(nh)], axis=-1).astype(BF16)
        o = jnp.dot(p[n * nh * length:(n + 1) * nh * length], vn, preferred_element_type=F32)
        for hh in range(nh):
            obuf[rows_p + n * length:rows_p + (n + 1) * length, hh * hd:(hh + 1) * hd] = (
                o[hh * length:(hh + 1) * length, hh * hd:(hh + 1) * hd])

    y = x + _rms(_bdot(obuf[...], w_o[...]), g_ref[3:4, :])
    yp_ref[...] = y[:rows_p]
    ys_ref[...] = y[rows_p:].reshape(nb, length, c)


def _attn(xp, xs, mem_k, mem_v, cache_k, cache_v, gains, i, w_q, w_o, length):
    b, s, c = xp.shape
    db, ls, _ = xs.shape
    nt = s // length
    nb = db // (b * nt)
    assert nb * b * nt == db
    lay = lambda bb, tt: (i, 0, 0)
    crow = mem_k.shape[2:]
    p_kv = pl.BlockSpec((None, None) + crow, lambda bb, tt: (i, bb, 0, 0))
    s_kv = pl.BlockSpec((None, nb) + crow, lambda bb, tt: (i, bb * nt + tt, 0, 0))
    xp_spec = pl.BlockSpec((None, length, c), lambda bb, tt: (bb, tt, 0))
    xs_spec = pl.BlockSpec((nb, ls, c), lambda bb, tt: (bb * nt + tt, 0, 0))
    return pl.pallas_call(
        _attn_kernel,
        grid=(b, nt),
        in_specs=[
            xp_spec, xs_spec, p_kv, p_kv, s_kv, s_kv,
            pl.BlockSpec((None, N_NORMS, c), lay),
            pl.BlockSpec((None, c, c), lay, pipeline_mode=pl.Buffered(1)),
            pl.BlockSpec((None, c, c), lay, pipeline_mode=pl.Buffered(1)),
        ],
        out_specs=[xp_spec, xs_spec],
        out_shape=[jax.ShapeDtypeStruct(xp.shape, F32), jax.ShapeDtypeStruct(xs.shape, F32)],
        scratch_shapes=[pltpu.VMEM((length + nb * ls, c), F32)],
        compiler_params=_params(),
        name="mem_cross_attn",
    )(xp, xs, mem_k, mem_v, cache_k, cache_v, gains, w_q, w_o)


def _ffn_kernel(sample, permute_out, *refs):
    x_ref, pre_ref, ins, outs, (act,), carry = _split_refs(refs, sample, 5, 2, 1)
    g_ref, w_up, w_dw, b_dw, w_down = ins
    y_ref, st_ref = outs
    t = pl.program_id(1)
    nt = pl.num_programs(1)
    if sample:
        r8, n_cur = x_ref.shape[:2]
    else:
        n_cur, r8 = x_ref.shape[0] // SUBLANES, SUBLANES
    rows = n_cur * r8
    c = x_ref.shape[-1]
    cw = FFN_CHUNK
    hist = FFN_PREFIX

    x = _load_rows(x_ref, sample, False)
    hb = _rms(x, g_ref[4:5, :]).astype(BF16)

    if not sample:
        @pl.when(t == 0)
        def _():
            carry[...] = jnp.zeros(carry.shape, F32)

    def conv(cols):
        u = jnp.dot(hb, w_up[:, cols], preferred_element_type=F32)
        u3 = u.reshape(n_cur, r8, cw)
        if sample:
            past = jnp.stack([pre_ref[:, j, cols] for j in range(hist)])
            for j in range(hist):
                st_ref[:, j, cols] = u3[n_cur - hist + j]
        else:
            last = u3[n_cur - hist:]
            past = _segment_history(last, carry[:, :, cols])
            carry[:, :, cols] = last
        e = jnp.concatenate([past.reshape(hist * r8, cw), u], axis=0)
        y = b_dw[:, cols] + w_dw[FFN_CONV_WIDTH - 1:FFN_CONV_WIDTH, cols] * u
        for k in range(hist):
            y = y + w_dw[k:k + 1, cols] * e[k * r8:k * r8 + rows]
        return y

    for ci in range(D_FF // cw):
        gcols = slice(ci * cw, (ci + 1) * cw)
        vcols = slice(D_FF + ci * cw, D_FF + (ci + 1) * cw)
        cg = conv(gcols)
        cv = conv(vcols)
        act[:, gcols] = (cg * _sigmoid(cg) * cv).astype(BF16)
    out = jnp.dot(act[...], w_down[...], preferred_element_type=F32)
    _store_rows(y_ref, x + _rms(out, g_ref[5:6, :]), sample, permute_out)

    if not sample:
        @pl.when(t == nt - 1)
        def _():
            st_ref[...] = carry[:, SUBLANES - 1, :]


def _ffn(x, prefix, gains, i, w_up, w_dw, b_dw, w_down, nb, length, permute_out):
    b, s, c = x.shape
    f2 = 2 * D_FF
    sample = prefix is not None
    assert (s == length) if sample else (nb == 1)
    lay = lambda bb, tt: (i, 0, 0)
    single = pl.Buffered(1)
    in_specs = [_x_specs(sample, nb, length, c)]
    args = [x]
    if sample:
        in_specs.append(pl.BlockSpec((None, nb, FFN_PREFIX, f2), lambda bb, tt: (i, bb, 0, 0)))
        args.append(prefix)
    in_specs += [
        pl.BlockSpec((None, N_NORMS, c), lay),
        pl.BlockSpec((None, c, f2), lay, pipeline_mode=single),
        pl.BlockSpec((None, FFN_CONV_WIDTH, f2), lay),
        pl.BlockSpec((None, 1, f2), lay),
        pl.BlockSpec((None, D_FF, c), lay, pipeline_mode=single),
    ]
    args += [gains, w_up, w_dw, b_dw, w_down]
    scratch = [pltpu.VMEM((nb * length, D_FF), BF16)]
    if not sample:
        scratch.append(pltpu.VMEM((FFN_PREFIX, SUBLANES, f2), F32))
    return pl.pallas_call(
        functools.partial(_ffn_kernel, sample, permute_out),
        grid=(b // nb, s // length),
        in_specs=in_specs,
        out_specs=[_x_specs(sample, nb, length, c), _state_spec(sample, nb, FFN_PREFIX, f2)],
        out_shape=[jax.ShapeDtypeStruct((b, s, c), F32), jax.ShapeDtypeStruct((b, FFN_PREFIX, f2), F32)],
        scratch_shapes=scratch,
        compiler_params=_params(),
        name="conv_ffn",
    )(*args)


def _mem_kv_kernel(m_ref, g_ref, w_kv, k_ref, v_ref):
    nbm, n_mem, c = m_ref.shape
    kv = _bdot(_rms(m_ref[...].reshape(nbm * n_mem, c), g_ref[N_NORMS - 1:N_NORMS, :]), w_kv[...])
    lane_tiles = MEM_HEAD_DIM // LANES
    stride = lane_tiles * N_MEM_HEADS
    for ref, base in ((k_ref, 0), (v_ref, c)):
        for bi in range(nbm):
            for hh in range(N_MEM_HEADS):
                for dt in range(lane_tiles):
                    col = base + hh * MEM_HEAD_DIM + dt * LANES
                    ref[bi, pl.ds(dt * N_MEM_HEADS + hh, n_mem, stride=stride), :] = (
                        kv[bi * n_mem:(bi + 1) * n_mem, col:col + LANES])


def _mem_kv(mem, gains, w_kv, nbm):
    b, n_mem, c = mem.shape
    crows = n_mem * (c // LANES)
    out_spec = pl.BlockSpec((None, nbm, crows, LANES), lambda ii, rr: (ii, rr, 0, 0))
    out_shape = jax.ShapeDtypeStruct((DEPTH, b, crows, LANES), F32)
    return pl.pallas_call(
        _mem_kv_kernel,
        grid=(DEPTH, b // nbm),
        in_specs=[
            pl.BlockSpec((nbm, n_mem, c), lambda ii, rr: (rr, 0, 0)),
            pl.BlockSpec((None, N_NORMS, c), lambda ii, rr: (ii, 0, 0)),
            pl.BlockSpec((None, c, 2 * c), lambda ii, rr: (ii, 0, 0)),
        ],
        out_specs=[out_spec, out_spec],
        out_shape=[out_shape, out_shape],
        compiler_params=_params(),
        name="mem_kv",
    )(mem, gains, w_kv)


_PLAN = {
    "prompt": dict(nb=1, length=PROMPT_TILE, conf_rc=8),
    "sample": dict(nb=32, length=8, conf_rc=8),
}


def _mixer(group, x, i, pos0, conv_pre, pool_pre, p):
    plan = _PLAN[group]
    nb, length = plan["nb"], plan["length"]
    j = i // 2
    if i % 2 == 0:
        return _conformer(x, conv_pre, j, p["gains"], i, p["a_w_in"], p["a_b_in"], p["a_w_dw"], p["a_b_dw"],
                          p["a_ln_g"], p["a_ln_b"], p["a_w_out"], p["a_b_out"], nb, length, plan["conf_rc"],
                          permute_in=(group == "prompt" and i == 0))
    return _pool(x, pool_pre, j, pos0, p["gains"], i, p["p_w_group"], p["p_scale"], nb, length)


def _conv_ffn(group, x, i, ffn_pre, p):
    plan = _PLAN[group]
    return _ffn(x, ffn_pre, p["gains"], i, p["f_w_up"], p["f_w_dw"], p["f_b_dw"], p["f_w_down"], plan["nb"],
                plan["length"], permute_out=(group == "prompt" and i == DEPTH - 1))


def _cache_rows(cache):
    d, b = cache.shape[:2]
    lane_tiles = MEM_HEAD_DIM // LANES
    tiled = cache.reshape(d, b, N_MEM, N_MEM_HEADS, lane_tiles, LANES).transpose(0, 1, 2, 4, 3, 5)
    return tiled.reshape(d, b, N_MEM * lane_tiles * N_MEM_HEADS, LANES)


def _cache_unrows(rows):
    d, b = rows.shape[:2]
    lane_tiles = MEM_HEAD_DIM // LANES
    tiled = rows.reshape(d, b, N_MEM, lane_tiles, N_MEM_HEADS, LANES).transpose(0, 1, 2, 4, 3, 5)
    return tiled.reshape(d, b, N_MEM, N_MEM_HEADS, MEM_HEAD_DIM)


def kernel(x_prompt, x_sample, mem_prompt, state_conv, state_pool, state_ffn, cache_mem_k, cache_mem_v, norm_gains, a_w_in, a_b_in, a_w_dw, a_b_dw, a_ln_g, a_ln_b, a_w_out, a_b_out, p_w_group, p_scale, c_w_q, c_w_kv, c_w_o, f_w_up, f_w_dw, f_b_dw, f_w_down):
    b = x_prompt.shape[0]
    c = D_MODEL
    p = dict(
        gains=norm_gains,
        a_w_in=a_w_in.astype(BF16), a_b_in=a_b_in[:, None, :], a_w_dw=a_w_dw, a_b_dw=a_b_dw[:, None, :],
        a_ln_g=a_ln_g[:, None, :], a_ln_b=a_ln_b[:, None, :], a_w_out=a_w_out.astype(BF16),
        a_b_out=a_b_out[:, None, :],
        p_w_group=p_w_group.astype(BF16), p_scale=p_scale[:, None, :],
        c_w_q=c_w_q.astype(BF16), c_w_o=c_w_o.astype(BF16),
        f_w_up=f_w_up.astype(BF16), f_w_dw=f_w_dw, f_b_dw=f_b_dw[:, None, :], f_w_down=f_w_down.astype(BF16),
    )
    mk, mv = _mem_kv(mem_prompt, norm_gains, c_w_kv.astype(BF16), 2)
    ck, cv = _cache_rows(cache_mem_k), _cache_rows(cache_mem_v)
    tm = lambda a: a.transpose(0, 2, 1, 3)
    conv_pre, pool_pre = tm(state_conv), tm(state_pool)
    xp, xs = x_prompt, x_sample
    mix_p, mix_s, ffn_p, ffn_s = [], [], [], []
    for i in range(DEPTH):
        xp, st = _mixer("prompt", xp, i, 0, None, None, p)
        mix_p.append(st)
        xs, st = _mixer("sample", xs, i, PAST_LEN, conv_pre, pool_pre, p)
        mix_s.append(st)
        xp, xs = _attn(xp, xs, mk, mv, ck, cv, norm_gains, i, p["c_w_q"], p["c_w_o"], PROMPT_TILE)
        xp, st = _conv_ffn("prompt", xp, i, None, p)
        ffn_p.append(st)
        xs, st = _conv_ffn("sample", xs, i, state_ffn, p)
        ffn_s.append(st)
    return (xp, xs, jnp.stack(mix_p[0::2]), jnp.stack(mix_p[1::2]), jnp.stack(ffn_p),
            _cache_unrows(mk), _cache_unrows(mv),
            tm(jnp.stack(mix_s[0::2])), tm(jnp.stack(mix_s[1::2])), jnp.stack(ffn_s))
```

```python
import functools

import jax
import jax.numpy as jnp
from jax import lax
from jax.experimental import pallas as pl
from jax.experimental.pallas import tpu as pltpu

D_MODEL = 1024
DEPTH = 4
PAST_LEN = 16384
CONV_WIDTH = 31
CONV_PREFIX = CONV_WIDTH - 1
POOL_WINDOWS = (2, 4, 8, 16)
POOL_GROUP_DIM = D_MODEL // len(POOL_WINDOWS)
POOL_PREFIX = max(POOL_WINDOWS) - 1
N_MEM = 256
N_MEM_HEADS = 4
MEM_HEAD_DIM = D_MODEL // N_MEM_HEADS
D_FF = 2816
FFN_CONV_WIDTH = 3
FFN_PREFIX = FFN_CONV_WIDTH - 1
N_NORMS = 7
RMS_EPS = 1e-6
LN_EPS = 1e-5

SUBLANES = 8
LANES = 128
PROMPT_TILE = 512
PROMPT_SEG = PROMPT_TILE // SUBLANES
FFN_CHUNK = 256
CONV_COLS = LANES
VMEM_LIMIT_BYTES = 56 * 1024 * 1024

BF16 = jnp.bfloat16
F32 = jnp.float32


def _rms(x, g):
    ms = jnp.mean(x * x, axis=-1, keepdims=True)
    return x * lax.rsqrt(ms + RMS_EPS) * g


def _bdot(a, w):
    return jnp.dot(a.astype(BF16), w, preferred_element_type=F32)


def _sigmoid(x):
    return 0.5 * jnp.tanh(0.5 * x) + 0.5


def _to_segments(x):
    rows, c = x.shape
    return x.reshape(SUBLANES, rows // SUBLANES, c).swapaxes(0, 1).reshape(rows, c)


def _from_segments(x):
    rows, c = x.shape
    return x.reshape(rows // SUBLANES, SUBLANES, c).swapaxes(0, 1).reshape(rows, c)


def _load_rows(x_ref, sample, permute_in):
    if sample:
        return jnp.concatenate([x_ref[:, t, :] for t in range(x_ref.shape[1])], axis=0)
    x = x_ref[...]
    return _to_segments(x) if permute_in else x


def _store_rows(y_ref, y, sample, permute_out):
    if sample:
        nb = y_ref.shape[0]
        for t in range(y_ref.shape[1]):
            y_ref[:, t, :] = y[t * nb:(t + 1) * nb]
    else:
        y_ref[...] = _from_segments(y) if permute_out else y


def _segment_history(last, carry_val):
    sub = lax.broadcasted_iota(jnp.int32, last.shape, 1)
    return pltpu.roll(jnp.where(sub == SUBLANES - 1, carry_val, last), 1, axis=1)


def _fill_history(ext, cur3, hist, carry, pre_ref, t, sample):
    n_cur = cur3.shape[0]
    if sample:
        ext[0:hist] = pre_ref[...]
    else:
        @pl.when(t == 0)
        def _():
            carry[...] = jnp.zeros(carry.shape, F32)

        last = cur3[n_cur - hist:]
        ext[0:hist] = _segment_history(last, carry[...])
        carry[...] = last
    ext[hist:hist + n_cur] = cur3


def _store_state(st_ref, ext, hist, carry, t, nt, sample):
    n_cur = ext.shape[0] - hist
    if sample:
        st_ref[...] = ext[n_cur:n_cur + hist]
    else:
        @pl.when(t == nt - 1)
        def _():
            st_ref[...] = carry[:, SUBLANES - 1, :]


def _split_refs(refs, sample, n_in, n_out, n_scratch):
    refs = list(refs)
    x_ref = refs.pop(0)
    pre_ref = refs.pop(0) if sample else None
    ins, refs = refs[:n_in], refs[n_in:]
    if sample == "stacked":
        refs.pop(0)
    outs, refs = refs[:n_out], refs[n_out:]
    scr, refs = refs[:n_scratch], refs[n_scratch:]
    carry = None if sample else refs.pop(0)
    assert not refs
    return x_ref, pre_ref, ins, outs, scr, carry


def _conformer_kernel(sample, permute_in, rc, *refs):
    x_ref, pre_ref, ins, outs, scr, carry = _split_refs(refs, sample, 9, 2, 2)
    g_ref, w_in, b_in, w_dw, b_dw, ln_g, ln_b, w_out, b_out = ins
    y_ref, st_ref = outs
    ext, cbuf = scr
    t = pl.program_id(1)
    nt = pl.num_programs(1)
    n_cur, r8, c = cbuf.shape
    rows = n_cur * r8

    x = _load_rows(x_ref, sample, permute_in)
    h = _rms(x, g_ref[0:1, :])
    u = _bdot(h, w_in[...]) + b_in[...]
    glu = u[:, :c] * _sigmoid(u[:, c:])
    _fill_history(ext, glu.reshape(n_cur, r8, c), CONV_PREFIX, carry, pre_ref, t, sample)

    def chunk(ci, carry_):
        i0 = ci * rc
        for g in range(r8 // SUBLANES):
            rsl = slice(g * SUBLANES, (g + 1) * SUBLANES)
            def lane_col(l, carry2):
                cols = pl.ds(pl.multiple_of(l * CONV_COLS, CONV_COLS), CONV_COLS)
                bias = jnp.broadcast_to(b_dw[:, cols], (SUBLANES, CONV_COLS))
                accs = [bias] * rc
                for m in range(rc + CONV_PREFIX):
                    tile = ext[i0 + m, rsl, cols]
                    for j in range(max(0, m - CONV_PREFIX), min(rc, m + 1)):
                        accs[j] = accs[j] + w_dw[m - j:m - j + 1, cols] * tile
                cbuf[pl.ds(i0, rc), rsl, cols] = jnp.stack(accs)
                return carry2

            lax.fori_loop(0, c // CONV_COLS, lane_col, 0)
        return carry_

    lax.fori_loop(0, n_cur // rc, chunk, 0)

    v = cbuf[...].reshape(rows, c)
    mu = jnp.mean(v, axis=-1, keepdims=True)
    xc = v - mu
    var = jnp.mean(xc * xc, axis=-1, keepdims=True)
    yn = xc * lax.rsqrt(var + LN_EPS) * ln_g[...] + ln_b[...]
    out = _bdot(yn * _sigmoid(yn), w_out[...]) + b_out[...]
    _store_rows(y_ref, x + _rms(out, g_ref[1:2, :]), sample, False)
    _store_state(st_ref, ext, CONV_PREFIX, carry, t, nt, sample)


def _x_specs(sample, nb, length, c):
    if sample:
        return pl.BlockSpec((nb, length, c), lambda bb, tt: (bb, 0, 0))
    return pl.BlockSpec((None, length, c), lambda bb, tt: (bb, tt, 0))


def _state_io(prefix, acc, layer, block, index, b, hist, width, in_specs, args):
    if prefix is None:
        spec = pl.BlockSpec((None, hist, width), lambda bb, tt: (bb, 0, 0))
        return False, spec, jax.ShapeDtypeStruct((b, hist, width), F32), {}
    spec = pl.BlockSpec((None,) + block, lambda bb, tt: (layer,) + index(bb))
    shape = jax.ShapeDtypeStruct(prefix.shape, F32)
    in_specs.append(pl.BlockSpec(memory_space=pl.ANY))
    args.append(jnp.zeros(prefix.shape, F32) if acc is None else acc)
    return "stacked", spec, shape, {len(args) - 1: 1}


def _time_tiles(sample, nb, length):
    return (length, nb) if sample else (length // SUBLANES, SUBLANES)


def _params(sem=("arbitrary", "arbitrary")):
    return pltpu.CompilerParams(dimension_semantics=sem, vmem_limit_bytes=VMEM_LIMIT_BYTES)


def _conformer(x, prefix, acc, j, gains, i, w_in, b_in, w_dw, b_dw, ln_g, ln_b, w_out, b_out, nb, length, rc,
               permute_in):
    b, s, c = x.shape
    sample = prefix is not None
    assert (s == length) if sample else (nb == 1)
    n_cur, r8 = _time_tiles(sample, nb, length)
    lay = lambda bb, tt: (i, 0, 0)
    layj = lambda bb, tt: (j, 0, 0)
    single = pl.Buffered(1)
    in_specs = [_x_specs(sample, nb, length, c)]
    args = [x]
    if sample:
        in_specs.append(pl.BlockSpec((None, CONV_PREFIX, nb, c), lambda bb, tt: (j, 0, bb, 0)))
        args.append(prefix)
    in_specs += [
        pl.BlockSpec((None, N_NORMS, c), lay),
        pl.BlockSpec((None, c, 2 * c), layj, pipeline_mode=single),
        pl.BlockSpec((None, 1, 2 * c), layj),
        pl.BlockSpec((None, CONV_WIDTH, c), layj),
        pl.BlockSpec((None, 1, c), layj),
        pl.BlockSpec((None, 1, c), layj),
        pl.BlockSpec((None, 1, c), layj),
        pl.BlockSpec((None, c, c), layj, pipeline_mode=single),
        pl.BlockSpec((None, 1, c), layj),
    ]
    args += [gains, w_in, b_in, w_dw, b_dw, ln_g, ln_b, w_out, b_out]
    mode, st_spec, st_shape, aliases = _state_io(prefix, acc, j, (CONV_PREFIX, nb, c), lambda bb: (0, bb, 0),
                                                 b, CONV_PREFIX, c, in_specs, args)
    scratch = [pltpu.VMEM((CONV_PREFIX + n_cur, r8, c), F32), pltpu.VMEM((n_cur, r8, c), F32)]
    if not sample:
        scratch.append(pltpu.VMEM((CONV_PREFIX, SUBLANES, c), F32))
    return pl.pallas_call(
        functools.partial(_conformer_kernel, mode, permute_in, rc),
        grid=(b // nb, s // length),
        in_specs=in_specs,
        out_specs=[_x_specs(sample, nb, length, c), st_spec],
        out_shape=[jax.ShapeDtypeStruct((b, s, c), F32), st_shape],
        scratch_shapes=scratch,
        input_output_aliases=aliases,
        compiler_params=_params(),
        name="conformer_mixer",
    )(*args)


def _pool_kernel(sample, pos0, *refs):
    x_ref, pre_ref, ins, outs, _, carry = _split_refs(refs, sample, 3, 2, 0)
    g_ref, w_grp, scale = ins
    y_ref, st_ref = outs
    t = pl.program_id(1)
    nt = pl.num_programs(1)
    if sample:
        r8, n_cur = x_ref.shape[:2]
    else:
        n_cur, r8 = x_ref.shape[0] // SUBLANES, SUBLANES
    c = x_ref.shape[-1]
    rows = n_cur * r8
    gd = POOL_GROUP_DIM
    hist = POOL_PREFIX

    x = _load_rows(x_ref, sample, False)
    h3 = _rms(x, g_ref[0:1, :]).reshape(n_cur, r8, c)
    if sample:
        past = pre_ref[...]
    else:
        @pl.when(t == 0)
        def _():
            carry[...] = jnp.zeros(carry.shape, F32)

        last = h3[n_cur - hist:]
        past = _segment_history(last, carry[...])
        carry[...] = last
    e = jnp.concatenate([past, h3], axis=0)

    step = lax.broadcasted_iota(jnp.int32, (n_cur, r8, 1), 0)
    if sample:
        pos = pos0 + step
    else:
        pos = pos0 + t * rows + lax.broadcasted_iota(jnp.int32, (n_cur, r8, 1), 1) * n_cur + step
    posf = pos.astype(F32)

    outs_g = []
    for gi, w in enumerate(POOL_WINDOWS):
        eg = e[:, :, gi * gd:(gi + 1) * gd]
        s, m = eg, 1
        while m < w:
            s = s[m:] + s[:-m]
            m *= 2
        first = hist - (w - 1)
        inv = 1.0 / jnp.minimum(jnp.float32(w), posf + 1.0)
        pooled = s[first:first + n_cur] * inv - eg[hist:]
        outs_g.append(_bdot(pooled.reshape(rows, gd), w_grp[gi]))
    out = jnp.concatenate(outs_g, axis=-1) * scale[...]
    _store_rows(y_ref, x + _rms(out, g_ref[1:2, :]), sample, False)
    if sample:
        st_ref[...] = e[n_cur:]
    else:
        @pl.when(t == nt - 1)
        def _():
            st_ref[...] = carry[:, SUBLANES - 1, :]


def _pool(x, prefix, acc, j, pos0, gains, i, w_grp, scale, nb, length):
    b, s, c = x.shape
    sample = prefix is not None
    assert (s == length) if sample else (nb == 1)
    in_specs = [_x_specs(sample, nb, length, c)]
    args = [x]
    if sample:
        in_specs.append(pl.BlockSpec((None, POOL_PREFIX, nb, c), lambda bb, tt: (j, 0, bb, 0)))
        args.append(prefix)
    in_specs += [
        pl.BlockSpec((None, N_NORMS, c), lambda bb, tt: (i, 0, 0)),
        pl.BlockSpec((None, len(POOL_WINDOWS), POOL_GROUP_DIM, POOL_GROUP_DIM), lambda bb, tt: (j, 0, 0, 0)),
        pl.BlockSpec((None, 1, c), lambda bb, tt: (j, 0, 0)),
    ]
    args += [gains, w_grp, scale]
    mode, st_spec, st_shape, aliases = _state_io(prefix, acc, j, (POOL_PREFIX, nb, c), lambda bb: (0, bb, 0),
                                                 b, POOL_PREFIX, c, in_specs, args)
    scratch = [] if sample else [pltpu.VMEM((POOL_PREFIX, SUBLANES, c), F32)]
    return pl.pallas_call(
        functools.partial(_pool_kernel, mode, pos0),
        grid=(b // nb, s // length),
        in_specs=in_specs,
        out_specs=[_x_specs(sample, nb, length, c), st_spec],
        out_shape=[jax.ShapeDtypeStruct((b, s, c), F32), st_shape],
        scratch_shapes=scratch,
        input_output_aliases=aliases,
        compiler_params=_params(),
        name="pool_mixer",
    )(*args)


def _attend(qh, kh, vh):
    s = lax.dot_general(qh, kh.astype(BF16), (((1,), (1,)), ((), ())), preferred_element_type=F32)
    e = jnp.exp(s - jnp.max(s, axis=-1, keepdims=True))
    p = e * (1.0 / jnp.sum(e, axis=-1, keepdims=True))
    return jnp.dot(p.astype(BF16), vh.astype(BF16), preferred_element_type=F32)


def _cache_head(ref, hh):
    lane_tiles = MEM_HEAD_DIM // LANES
    stride = lane_tiles * N_MEM_HEADS
    parts = [ref[pl.ds(dt * N_MEM_HEADS + hh, N_MEM, stride=stride), :] for dt in range(lane_tiles)]
    return jnp.concatenate(parts, axis=-1)


def _attn_kernel(xp_ref, xs_ref, kp_ref, vp_ref, ks_ref, vs_ref, g_ref, w_q, w_o, yp_ref, ys_ref, obuf):
    rows_p, c = xp_ref.shape
    nb, length, _ = xs_ref.shape
    hd = MEM_HEAD_DIM
    nh = N_MEM_HEADS
    x = jnp.concatenate([xp_ref[...], xs_ref[...].reshape(nb * length, c)], axis=0)
    q = _bdot(_rms(x, g_ref[2:3, :]), w_q[...]) * (hd ** -0.5)

    qp = q[:rows_p].astype(BF16)
    for hh in range(nh):
        cols = slice(hh * hd, (hh + 1) * hd)
        obuf[0:rows_p, cols] = _attend(qp[:, cols], _cache_head(kp_ref, hh), _cache_head(vp_ref, hh))

    qs = q[rows_p:]
    col_head = lax.broadcasted_iota(jnp.int32, qs.shape, 1) // hd
    q_heads = [jnp.where(col_head == hh, qs, 0.0).astype(BF16) for hh in range(nh)]
    scores = []
    for n in range(nb):
        rsl = slice(n * length, (n + 1) * length)
        qn = jnp.concatenate([qh[rsl] for qh in q_heads], axis=0)
        kn = jnp.concatenate([_cache_head(ks_ref.at[n], hh) for hh in range(nh)], axis=-1).astype(BF16)
        scores.append(lax.dot_general(qn, kn, (((1,), (1,)), ((), ())), preferred_element_type=F32))
    s = jnp.concatenate(scores, axis=0)
    e = jnp.exp(s - jnp.max(s, axis=-1, keepdims=True))
    p = (e * (1.0 / jnp.sum(e, axis=-1, keepdims=True))).astype(BF16)
    for n in range(nb):
        vn = jnp.concatenate([_cache_head(vs_ref.at[n], hh) for hh in range(nh)], axis=-1).astype(BF16)
        o = jnp.dot(p[n * nh * length:(n + 1) * nh * length], vn, preferred_element_type=F32)
        for hh in range(nh):
            obuf[rows_p + n * length:rows_p + (n + 1) * length, hh * hd:(hh + 1) * hd] = (
                o[hh * length:(hh + 1) * length, hh * hd:(hh + 1) * hd])

    y = x + _rms(_bdot(obuf[...], w_o[...]), g_ref[3:4, :])
    yp_ref[...] = y[:rows_p]
    ys_ref[...] = y[rows_p:].reshape(nb, length, c)


def _attn(xp, xs, mem_k, mem_v, cache_k, cache_v, gains, i, w_q, w_o, length):
    b, s, c = xp.shape
    db, ls, _ = xs.shape
    nt = s // length
    nb = db // (b * nt)
    assert nb * b * nt == db
    lay = lambda bb, tt: (i, 0, 0)
    crow = mem_k.shape[2:]
    p_kv = pl.BlockSpec((None, None) + crow, lambda bb, tt: (i, bb, 0, 0))
    s_kv = pl.BlockSpec((None, nb) + crow, lambda bb, tt: (i, bb * nt + tt, 0, 0))
    xp_spec = pl.BlockSpec((None, length, c), lambda bb, tt: (bb, tt, 0))
    xs_spec = pl.BlockSpec((nb, ls, c), lambda bb, tt: (bb * nt + tt, 0, 0))
    return pl.pallas_call(
        _attn_kernel,
        grid=(b, nt),
        in_specs=[
            xp_spec, xs_spec, p_kv, p_kv, s_kv, s_kv,
            pl.BlockSpec((None, N_NORMS, c), lay),
            pl.BlockSpec((None, c, c), lay, pipeline_mode=pl.Buffered(1)),
            pl.BlockSpec((None, c, c), lay, pipeline_mode=pl.Buffered(1)),
        ],
        out_specs=[xp_spec, xs_spec],
        out_shape=[jax.ShapeDtypeStruct(xp.shape, F32), jax.ShapeDtypeStruct(xs.shape, F32)],
        scratch_shapes=[pltpu.VMEM((length + nb * ls, c), F32)],
        compiler_params=_params(),
        name="mem_cross_attn",
    )(xp, xs, mem_k, mem_v, cache_k, cache_v, gains, w_q, w_o)


def _ffn_kernel(sample, permute_out, *refs):
    x_ref, pre_ref, ins, outs, (act,), carry = _split_refs(refs, sample, 5, 2, 1)
    g_ref, w_up, w_dw, b_dw, w_down = ins
    y_ref, st_ref = outs
    t = pl.program_id(1)
    nt = pl.num_programs(1)
    if sample:
        r8, n_cur = x_ref.shape[:2]
    else:
        n_cur, r8 = x_ref.shape[0] // SUBLANES, SUBLANES
    rows = n_cur * r8
    c = x_ref.shape[-1]
    cw = FFN_CHUNK
    hist = FFN_PREFIX

    x = _load_rows(x_ref, sample, False)
    hb = _rms(x, g_ref[4:5, :]).astype(BF16)

    if not sample:
        @pl.when(t == 0)
        def _():
            carry[...] = jnp.zeros(carry.shape, F32)

    def conv(cols):
        u = jnp.dot(hb, w_up[:, cols], preferred_element_type=F32)
        u3 = u.reshape(n_cur, r8, cw)
        if sample:
            past = jnp.stack([pre_ref[:, j, cols] for j in range(hist)])
            for j in range(hist):
                st_ref[:, j, cols] = u3[n_cur - hist + j]
        else:
            last = u3[n_cur - hist:]
            past = _segment_history(last, carry[:, :, cols])
            carry[:, :, cols] = last
        e = jnp.concatenate([past.reshape(hist * r8, cw), u], axis=0)
        y = b_dw[:, cols] + w_dw[FFN_CONV_WIDTH - 1:FFN_CONV_WIDTH, cols] * u
        for k in range(hist):
            y = y + w_dw[k:k + 1, cols] * e[k * r8:k * r8 + rows]
        return y

    for ci in range(D_FF // cw):
        gcols = slice(ci * cw, (ci + 1) * cw)
        vcols = slice(D_FF + ci * cw, D_FF + (ci + 1) * cw)
        cg = conv(gcols)
        cv = conv(vcols)
        act[:, gcols] = (cg * _sigmoid(cg) * cv).astype(BF16)
    out = jnp.dot(act[...], w_down[...], preferred_element_type=F32)
    _store_rows(y_ref, x + _rms(out, g_ref[5:6, :]), sample, permute_out)

    if not sample:
        @pl.when(t == nt - 1)
        def _():
            st_ref[...] = carry[:, SUBLANES - 1, :]


def _ffn(x, prefix, acc, gains, i, w_up, w_dw, b_dw, w_down, nb, length, permute_out):
    b, s, c = x.shape
    f2 = 2 * D_FF
    sample = prefix is not None
    assert (s == length) if sample else (nb == 1)
    lay = lambda bb, tt: (i, 0, 0)
    single = pl.Buffered(1)
    in_specs = [_x_specs(sample, nb, length, c)]
    args = [x]
    if sample:
        in_specs.append(pl.BlockSpec((None, nb, FFN_PREFIX, f2), lambda bb, tt: (i, bb, 0, 0)))
        args.append(prefix)
    in_specs += [
        pl.BlockSpec((None, N_NORMS, c), lay),
        pl.BlockSpec((None, c, f2), lay, pipeline_mode=single),
        pl.BlockSpec((None, FFN_CONV_WIDTH, f2), lay),
        pl.BlockSpec((None, 1, f2), lay),
        pl.BlockSpec((None, D_FF, c), lay, pipeline_mode=single),
    ]
    args += [gains, w_up, w_dw, b_dw, w_down]
    mode, st_spec, st_shape, aliases = _state_io(prefix, acc, i, (nb, FFN_PREFIX, f2), lambda bb: (bb, 0, 0),
                                                 b, FFN_PREFIX, f2, in_specs, args)
    scratch = [pltpu.VMEM((nb * length, D_FF), BF16)]
    if not sample:
        scratch.append(pltpu.VMEM((FFN_PREFIX, SUBLANES, f2), F32))
    return pl.pallas_call(
        functools.partial(_ffn_kernel, mode, permute_out),
        grid=(b // nb, s // length),
        in_specs=in_specs,
        out_specs=[_x_specs(sample, nb, length, c), st_spec],
        out_shape=[jax.ShapeDtypeStruct((b, s, c), F32), st_shape],
        scratch_shapes=scratch,
        input_output_aliases=aliases,
        compiler_params=_params(),
        name="conv_ffn",
    )(*args)


def _mem_kv_kernel(n_cast, m_ref, g_ref, w_kv, *refs):
    src, (k_ref, v_ref), dst = refs[:n_cast], refs[n_cast:n_cast + 2], refs[n_cast + 2:]
    for s_ref, d_ref in zip(src, dst):
        d_ref[...] = s_ref[...].astype(BF16)
    nbm, n_mem, c = m_ref.shape
    kv = _bdot(_rms(m_ref[...].reshape(nbm * n_mem, c), g_ref[N_NORMS - 1:N_NORMS, :]), w_kv[...].astype(BF16))
    lane_tiles = MEM_HEAD_DIM // LANES
    stride = lane_tiles * N_MEM_HEADS
    for ref, base in ((k_ref, 0), (v_ref, c)):
        for bi in range(nbm):
            for hh in range(N_MEM_HEADS):
                for dt in range(lane_tiles):
                    col = base + hh * MEM_HEAD_DIM + dt * LANES
                    ref[bi, pl.ds(dt * N_MEM_HEADS + hh, n_mem, stride=stride), :] = (
                        kv[bi * n_mem:(bi + 1) * n_mem, col:col + LANES])


def _mem_kv(mem, gains, w_kv, to_cast, nbm):
    b, n_mem, c = mem.shape
    crows = n_mem * (c // LANES)
    steps = b // nbm
    out_spec = pl.BlockSpec((None, nbm, crows, LANES), lambda ii, rr: (ii, rr, 0, 0))
    out_shape = jax.ShapeDtypeStruct((DEPTH, b, crows, LANES), F32)
    slabs = [pl.BlockSpec((None, w.shape[1] // steps, w.shape[2]), lambda ii, rr: (ii, rr, 0)) for w in to_cast]
    return pl.pallas_call(
        functools.partial(_mem_kv_kernel, len(to_cast)),
        grid=(DEPTH, steps),
        in_specs=[
            pl.BlockSpec((nbm, n_mem, c), lambda ii, rr: (rr, 0, 0)),
            pl.BlockSpec((None, N_NORMS, c), lambda ii, rr: (ii, 0, 0)),
            pl.BlockSpec((None, c, 2 * c), lambda ii, rr: (ii, 0, 0), pipeline_mode=pl.Buffered(1)),
        ] + slabs,
        out_specs=[out_spec, out_spec] + slabs,
        out_shape=[out_shape, out_shape] + [jax.ShapeDtypeStruct(w.shape, BF16) for w in to_cast],
        compiler_params=_params(),
        name="mem_kv",
    )(mem, gains, w_kv, *to_cast)


_PLAN = {
    "prompt": dict(nb=1, length=PROMPT_TILE, conf_rc=8),
    "sample": dict(nb=32, length=8, conf_rc=8),
}


def _mixer(group, x, i, pos0, pre, acc, p):
    plan = _PLAN[group]
    nb, length = plan["nb"], plan["length"]
    j = i // 2
    if i % 2 == 0:
        return _conformer(x, pre, acc, j, p["gains"], i, p["a_w_in"], p["a_b_in"], p["a_w_dw"], p["a_b_dw"],
                          p["a_ln_g"], p["a_ln_b"], p["a_w_out"], p["a_b_out"], nb, length, plan["conf_rc"],
                          permute_in=(group == "prompt" and i == 0))
    return _pool(x, pre, acc, j, pos0, p["gains"], i, p["p_w_group"], p["p_scale"], nb, length)


def _conv_ffn(group, x, i, ffn_pre, acc, p):
    plan = _PLAN[group]
    return _ffn(x, ffn_pre, acc, p["gains"], i, p["f_w_up"], p["f_w_dw"], p["f_b_dw"], p["f_w_down"], plan["nb"],
                plan["length"], permute_out=(group == "prompt" and i == DEPTH - 1))


def _cache_rows(cache):
    d, b = cache.shape[:2]
    lane_tiles = MEM_HEAD_DIM // LANES
    tiled = cache.reshape(d, b, N_MEM, N_MEM_HEADS, lane_tiles, LANES).transpose(0, 1, 2, 4, 3, 5)
    return tiled.reshape(d, b, N_MEM * lane_tiles * N_MEM_HEADS, LANES)


def _cache_unrows(rows):
    d, b = rows.shape[:2]
    lane_tiles = MEM_HEAD_DIM // LANES
    tiled = rows.reshape(d, b, N_MEM, lane_tiles, N_MEM_HEADS, LANES).transpose(0, 1, 2, 4, 3, 5)
    return tiled.reshape(d, b, N_MEM, N_MEM_HEADS, MEM_HEAD_DIM)


def kernel(x_prompt, x_sample, mem_prompt, state_conv, state_pool, state_ffn, cache_mem_k, cache_mem_v, norm_gains, a_w_in, a_b_in, a_w_dw, a_b_dw, a_ln_g, a_ln_b, a_w_out, a_b_out, p_w_group, p_scale, c_w_q, c_w_kv, c_w_o, f_w_up, f_w_dw, f_b_dw, f_w_down):
    b = x_prompt.shape[0]
    c = D_MODEL
    mk, mv, w_q_b, w_o_b, w_up_b, w_down_b = _mem_kv(mem_prompt, norm_gains, c_w_kv,
                                                     [c_w_q, c_w_o, f_w_up, f_w_down], 1)
    p = dict(
        gains=norm_gains,
        a_w_in=a_w_in.astype(BF16), a_b_in=a_b_in[:, None, :], a_w_dw=a_w_dw, a_b_dw=a_b_dw[:, None, :],
        a_ln_g=a_ln_g[:, None, :], a_ln_b=a_ln_b[:, None, :], a_w_out=a_w_out.astype(BF16),
        a_b_out=a_b_out[:, None, :],
        p_w_group=p_w_group.astype(BF16), p_scale=p_scale[:, None, :],
        c_w_q=w_q_b, c_w_o=w_o_b,
        f_w_up=w_up_b, f_w_dw=f_w_dw, f_b_dw=f_b_dw[:, None, :], f_w_down=w_down_b,
    )
    ck, cv = _cache_rows(cache_mem_k), _cache_rows(cache_mem_v)
    tm = lambda a: a.transpose(0, 2, 1, 3)
    conv_pre, pool_pre = tm(state_conv), tm(state_pool)
    xp, xs = x_prompt, x_sample
    mix_p, ffn_p = [], []
    mix_pre = (conv_pre, pool_pre)
    mix_s = [None, None]
    ffn_s = None
    for i in range(DEPTH):
        kind = i % 2
        xp, st = _mixer("prompt", xp, i, 0, None, None, p)
        mix_p.append(st)
        xs, mix_s[kind] = _mixer("sample", xs, i, PAST_LEN, mix_pre[kind], mix_s[kind], p)
        xp, xs = _attn(xp, xs, mk, mv, ck, cv, norm_gains, i, p["c_w_q"], p["c_w_o"], PROMPT_TILE)
        xp, st = _conv_ffn("prompt", xp, i, None, None, p)
        ffn_p.append(st)
        xs, ffn_s = _conv_ffn("sample", xs, i, state_ffn, ffn_s, p)
    return (xp, xs, jnp.stack(mix_p[0::2]), jnp.stack(mix_p[1::2]), jnp.stack(ffn_p),
            _cache_unrows(mk), _cache_unrows(mv), tm(mix_s[0]), tm(mix_s[1]), ffn_s)
```

```python
import functools

import jax
import jax.numpy as jnp
from jax import lax
from jax.experimental import pallas as pl
from jax.experimental.pallas import tpu as pltpu

D_MODEL = 1024
DEPTH = 4
PAST_LEN = 16384
CONV_WIDTH = 31
CONV_PREFIX = CONV_WIDTH - 1
POOL_WINDOWS = (2, 4, 8, 16)
POOL_GROUP_DIM = D_MODEL // len(POOL_WINDOWS)
POOL_PREFIX = max(POOL_WINDOWS) - 1
N_MEM = 256
N_MEM_HEADS = 4
MEM_HEAD_DIM = D_MODEL // N_MEM_HEADS
D_FF = 2816
FFN_CONV_WIDTH = 3
FFN_PREFIX = FFN_CONV_WIDTH - 1
N_NORMS = 7
RMS_EPS = 1e-6
LN_EPS = 1e-5

SUBLANES = 8
LANES = 128
PROMPT_TILE = 512
PROMPT_SEG = PROMPT_TILE // SUBLANES
FFN_CHUNK = 256
CONV_COLS = LANES
VMEM_LIMIT_BYTES = 56 * 1024 * 1024

BF16 = jnp.bfloat16
F32 = jnp.float32


def _rms(x, g):
    ms = jnp.mean(x * x, axis=-1, keepdims=True)
    return x * lax.rsqrt(ms + RMS_EPS) * g


def _bdot(a, w):
    return jnp.dot(a.astype(BF16), w, preferred_element_type=F32)


def _sigmoid(x):
    return 0.5 * jnp.tanh(0.5 * x) + 0.5


def _to_segments(x):
    rows, c = x.shape
    return x.reshape(SUBLANES, rows // SUBLANES, c).swapaxes(0, 1).reshape(rows, c)


def _from_segments(x):
    rows, c = x.shape
    tiles = [x[r:r + PROMPT_TILE].reshape(PROMPT_SEG, SUBLANES, c).swapaxes(0, 1).reshape(PROMPT_TILE, c)
             for r in range(0, rows, PROMPT_TILE)]
    return tiles[0] if len(tiles) == 1 else jnp.concatenate(tiles, axis=0)


def _load_rows(x_ref, sample, permute_in):
    if sample:
        return jnp.concatenate([x_ref[:, t, :] for t in range(x_ref.shape[1])], axis=0)
    x = x_ref[...]
    return _to_segments(x) if permute_in else x


def _store_rows(y_ref, y, sample, permute_out):
    if sample:
        nb = y_ref.shape[0]
        for t in range(y_ref.shape[1]):
            y_ref[:, t, :] = y[t * nb:(t + 1) * nb]
    else:
        y_ref[...] = _from_segments(y) if permute_out else y


def _segment_history(last, carry_val):
    sub = lax.broadcasted_iota(jnp.int32, last.shape, 1)
    return pltpu.roll(jnp.where(sub == SUBLANES - 1, carry_val, last), 1, axis=1)


def _fill_history(ext, cur3, hist, carry, pre_ref, t, sample):
    n_cur = cur3.shape[0]
    if sample:
        ext[0:hist] = pre_ref[...]
    else:
        @pl.when(t == 0)
        def _():
            carry[...] = jnp.zeros(carry.shape, F32)

        last = cur3[n_cur - hist:]
        ext[0:hist] = _segment_history(last, carry[...])
        carry[...] = last
    ext[hist:hist + n_cur] = cur3


def _store_state(st_ref, ext, hist, carry, t, nt, sample):
    n_cur = ext.shape[0] - hist
    if sample:
        st_ref[...] = ext[n_cur:n_cur + hist]
    else:
        @pl.when(t == nt - 1)
        def _():
            st_ref[...] = carry[:, SUBLANES - 1, :]


def _split_refs(refs, sample, n_in, n_out, n_scratch):
    refs = list(refs)
    x_ref = refs.pop(0)
    pre_ref = refs.pop(0) if sample else None
    ins, refs = refs[:n_in], refs[n_in:]
    if sample == "stacked":
        refs.pop(0)
    outs, refs = refs[:n_out], refs[n_out:]
    scr, refs = refs[:n_scratch], refs[n_scratch:]
    carry = None if sample else refs.pop(0)
    assert not refs
    return x_ref, pre_ref, ins, outs, scr, carry


def _conformer_kernel(sample, permute_in, rc, *refs):
    x_ref, pre_ref, ins, outs, scr, carry = _split_refs(refs, sample, 9, 2, 2)
    g_ref, w_in, b_in, w_dw, b_dw, ln_g, ln_b, w_out, b_out = ins
    y_ref, st_ref = outs
    ext, cbuf = scr
    t = pl.program_id(1)
    nt = pl.num_programs(1)
    n_cur, r8, c = cbuf.shape
    rows = n_cur * r8

    x = _load_rows(x_ref, sample, permute_in)
    h = _rms(x, g_ref[0:1, :])
    u = _bdot(h, w_in[...]) + b_in[...]
    glu = u[:, :c] * _sigmoid(u[:, c:])
    _fill_history(ext, glu.reshape(n_cur, r8, c), CONV_PREFIX, carry, pre_ref, t, sample)

    def chunk(ci, carry_):
        i0 = ci * rc
        for g in range(r8 // SUBLANES):
            rsl = slice(g * SUBLANES, (g + 1) * SUBLANES)
            def lane_col(l, carry2):
                cols = pl.ds(pl.multiple_of(l * CONV_COLS, CONV_COLS), CONV_COLS)
                bias = jnp.broadcast_to(b_dw[:, cols], (SUBLANES, CONV_COLS))
                accs = [bias] * rc
                for m in range(rc + CONV_PREFIX):
                    tile = ext[i0 + m, rsl, cols]
                    for j in range(max(0, m - CONV_PREFIX), min(rc, m + 1)):
                        accs[j] = accs[j] + w_dw[m - j:m - j + 1, cols] * tile
                cbuf[pl.ds(i0, rc), rsl, cols] = jnp.stack(accs)
                return carry2

            lax.fori_loop(0, c // CONV_COLS, lane_col, 0)
        return carry_

    lax.fori_loop(0, n_cur // rc, chunk, 0)

    v = cbuf[...].reshape(rows, c)
    mu = jnp.mean(v, axis=-1, keepdims=True)
    xc = v - mu
    var = jnp.mean(xc * xc, axis=-1, keepdims=True)
    yn = xc * lax.rsqrt(var + LN_EPS) * ln_g[...] + ln_b[...]
    out = _bdot(yn * _sigmoid(yn), w_out[...]) + b_out[...]
    _store_rows(y_ref, x + _rms(out, g_ref[1:2, :]), sample, False)
    _store_state(st_ref, ext, CONV_PREFIX, carry, t, nt, sample)


def _x_specs(sample, nb, length, c):
    if sample:
        return pl.BlockSpec((nb, length, c), lambda bb, tt: (bb, 0, 0))
    return pl.BlockSpec((None, length, c), lambda bb, tt: (bb, tt, 0))


def _state_io(prefix, acc, layer, block, index, b, hist, width, in_specs, args):
    if prefix is None:
        spec = pl.BlockSpec((None, hist, width), lambda bb, tt: (bb, 0, 0))
        return False, spec, jax.ShapeDtypeStruct((b, hist, width), F32), {}
    spec = pl.BlockSpec((None,) + block, lambda bb, tt: (layer,) + index(bb))
    shape = jax.ShapeDtypeStruct(prefix.shape, F32)
    in_specs.append(pl.BlockSpec(memory_space=pl.ANY))
    args.append(jnp.zeros(prefix.shape, F32) if acc is None else acc)
    return "stacked", spec, shape, {len(args) - 1: 1}


def _time_tiles(sample, nb, length):
    return (length, nb) if sample else (length // SUBLANES, SUBLANES)


def _params(sem=("arbitrary", "arbitrary")):
    return pltpu.CompilerParams(dimension_semantics=sem, vmem_limit_bytes=VMEM_LIMIT_BYTES)


def _conformer(x, prefix, acc, j, gains, i, w_in, b_in, w_dw, b_dw, ln_g, ln_b, w_out, b_out, nb, length, rc,
               permute_in):
    b, s, c = x.shape
    sample = prefix is not None
    assert (s == length) if sample else (nb == 1)
    n_cur, r8 = _time_tiles(sample, nb, length)
    lay = lambda bb, tt: (i, 0, 0)
    layj = lambda bb, tt: (j, 0, 0)
    single = pl.Buffered(1)
    in_specs = [_x_specs(sample, nb, length, c)]
    args = [x]
    if sample:
        in_specs.append(pl.BlockSpec((None, CONV_PREFIX, nb, c), lambda bb, tt: (j, 0, bb, 0)))
        args.append(prefix)
    in_specs += [
        pl.BlockSpec((None, N_NORMS, c), lay),
        pl.BlockSpec((None, c, 2 * c), layj, pipeline_mode=single),
        pl.BlockSpec((None, 1, 2 * c), layj),
        pl.BlockSpec((None, CONV_WIDTH, c), layj),
        pl.BlockSpec((None, 1, c), layj),
        pl.BlockSpec((None, 1, c), layj),
        pl.BlockSpec((None, 1, c), layj),
        pl.BlockSpec((None, c, c), layj, pipeline_mode=single),
        pl.BlockSpec((None, 1, c), layj),
    ]
    args += [gains, w_in, b_in, w_dw, b_dw, ln_g, ln_b, w_out, b_out]
    mode, st_spec, st_shape, aliases = _state_io(prefix, acc, j, (CONV_PREFIX, nb, c), lambda bb: (0, bb, 0),
                                                 b, CONV_PREFIX, c, in_specs, args)
    scratch = [pltpu.VMEM((CONV_PREFIX + n_cur, r8, c), F32), pltpu.VMEM((n_cur, r8, c), F32)]
    if not sample:
        scratch.append(pltpu.VMEM((CONV_PREFIX, SUBLANES, c), F32))
    return pl.pallas_call(
        functools.partial(_conformer_kernel, mode, permute_in, rc),
        grid=(b // nb, s // length),
        in_specs=in_specs,
        out_specs=[_x_specs(sample, nb, length, c), st_spec],
        out_shape=[jax.ShapeDtypeStruct((b, s, c), F32), st_shape],
        scratch_shapes=scratch,
        input_output_aliases=aliases,
        compiler_params=_params(),
        name="conformer_mixer",
    )(*args)


def _pool_kernel(sample, pos0, *refs):
    x_ref, pre_ref, ins, outs, _, carry = _split_refs(refs, sample, 3, 2, 0)
    g_ref, w_grp, scale = ins
    y_ref, st_ref = outs
    t = pl.program_id(1)
    nt = pl.num_programs(1)
    if sample:
        r8, n_cur = x_ref.shape[:2]
    else:
        n_cur, r8 = x_ref.shape[0] // SUBLANES, SUBLANES
    c = x_ref.shape[-1]
    rows = n_cur * r8
    gd = POOL_GROUP_DIM
    hist = POOL_PREFIX

    x = _load_rows(x_ref, sample, False)
    h3 = _rms(x, g_ref[0:1, :]).reshape(n_cur, r8, c)
    if sample:
        past = pre_ref[...]
    else:
        @pl.when(t == 0)
        def _():
            carry[...] = jnp.zeros(carry.shape, F32)

        last = h3[n_cur - hist:]
        past = _segment_history(last, carry[...])
        carry[...] = last
    e = jnp.concatenate([past, h3], axis=0)

    step = lax.broadcasted_iota(jnp.int32, (n_cur, r8, 1), 0)
    if sample:
        pos = pos0 + step
    else:
        pos = pos0 + t * rows + lax.broadcasted_iota(jnp.int32, (n_cur, r8, 1), 1) * n_cur + step
    posf = pos.astype(F32)

    outs_g = []
    for gi, w in enumerate(POOL_WINDOWS):
        eg = e[:, :, gi * gd:(gi + 1) * gd]
        s, m = eg, 1
        while m < w:
            s = s[m:] + s[:-m]
            m *= 2
        first = hist - (w - 1)
        inv = 1.0 / jnp.minimum(jnp.float32(w), posf + 1.0)
        pooled = s[first:first + n_cur] * inv - eg[hist:]
        outs_g.append(_bdot(pooled.reshape(rows, gd), w_grp[gi]))
    out = jnp.concatenate(outs_g, axis=-1) * scale[...]
    _store_rows(y_ref, x + _rms(out, g_ref[1:2, :]), sample, False)
    if sample:
        st_ref[...] = e[n_cur:]
    else:
        @pl.when(t == nt - 1)
        def _():
            st_ref[...] = carry[:, SUBLANES - 1, :]


def _pool(x, prefix, acc, j, pos0, gains, i, w_grp, scale, nb, length):
    b, s, c = x.shape
    sample = prefix is not None
    assert (s == length) if sample else (nb == 1)
    in_specs = [_x_specs(sample, nb, length, c)]
    args = [x]
    if sample:
        in_specs.append(pl.BlockSpec((None, POOL_PREFIX, nb, c), lambda bb, tt: (j, 0, bb, 0)))
        args.append(prefix)
    in_specs += [
        pl.BlockSpec((None, N_NORMS, c), lambda bb, tt: (i, 0, 0)),
        pl.BlockSpec((None, len(POOL_WINDOWS), POOL_GROUP_DIM, POOL_GROUP_DIM), lambda bb, tt: (j, 0, 0, 0)),
        pl.BlockSpec((None, 1, c), lambda bb, tt: (j, 0, 0)),
    ]
    args += [gains, w_grp, scale]
    mode, st_spec, st_shape, aliases = _state_io(prefix, acc, j, (POOL_PREFIX, nb, c), lambda bb: (0, bb, 0),
                                                 b, POOL_PREFIX, c, in_specs, args)
    scratch = [] if sample else [pltpu.VMEM((POOL_PREFIX, SUBLANES, c), F32)]
    return pl.pallas_call(
        functools.partial(_pool_kernel, mode, pos0),
        grid=(b // nb, s // length),
        in_specs=in_specs,
        out_specs=[_x_specs(sample, nb, length, c), st_spec],
        out_shape=[jax.ShapeDtypeStruct((b, s, c), F32), st_shape],
        scratch_shapes=scratch,
        input_output_aliases=aliases,
        compiler_params=_params(),
        name="pool_mixer",
    )(*args)


def _attend(qh, kh, vh):
    s = lax.dot_general(qh, kh.astype(BF16), (((1,), (1,)), ((), ())), preferred_element_type=F32)
    e = jnp.exp(s - jnp.max(s, axis=-1, keepdims=True))
    p = e * (1.0 / jnp.sum(e, axis=-1, keepdims=True))
    return jnp.dot(p.astype(BF16), vh.astype(BF16), preferred_element_type=F32)


def _cache_head(ref, hh):
    lane_tiles = MEM_HEAD_DIM // LANES
    stride = lane_tiles * N_MEM_HEADS
    parts = [ref[pl.ds(dt * N_MEM_HEADS + hh, N_MEM, stride=stride), :] for dt in range(lane_tiles)]
    return jnp.concatenate(parts, axis=-1)


def _attn_kernel(xp_ref, xs_ref, kp_ref, vp_ref, ks_ref, vs_ref, g_ref, w_q, w_o, yp_ref, ys_ref, obuf):
    rows_p, c = xp_ref.shape
    nb, length, _ = xs_ref.shape
    hd = MEM_HEAD_DIM
    nh = N_MEM_HEADS
    x = jnp.concatenate([xp_ref[...], xs_ref[...].reshape(nb * length, c)], axis=0)
    q = _bdot(_rms(x, g_ref[2:3, :]), w_q[...]) * (hd ** -0.5)

    qp = q[:rows_p].astype(BF16)
    for hh in range(nh):
        cols = slice(hh * hd, (hh + 1) * hd)
        obuf[0:rows_p, cols] = _attend(qp[:, cols], _cache_head(kp_ref, hh), _cache_head(vp_ref, hh))

    qs = q[rows_p:]
    col_head = lax.broadcasted_iota(jnp.int32, qs.shape, 1) // hd
    q_heads = [jnp.where(col_head == hh, qs, 0.0).astype(BF16) for hh in range(nh)]
    scores = []
    for n in range(nb):
        rsl = slice(n * length, (n + 1) * length)
        qn = jnp.concatenate([qh[rsl] for qh in q_heads], axis=0)
        kn = jnp.concatenate([_cache_head(ks_ref.at[n], hh) for hh in range(nh)], axis=-1).astype(BF16)
        scores.append(lax.dot_general(qn, kn, (((1,), (1,)), ((), ())), preferred_element_type=F32))
    s = jnp.concatenate(scores, axis=0)
    e = jnp.exp(s - jnp.max(s, axis=-1, keepdims=True))
    p = (e * (1.0 / jnp.sum(e, axis=-1, keepdims=True))).astype(BF16)
    for n in range(nb):
        vn = jnp.concatenate([_cache_head(vs_ref.at[n], hh) for hh in range(nh)], axis=-1).astype(BF16)
        o = jnp.dot(p[n * nh * length:(n + 1) * nh * length], vn, preferred_element_type=F32)
        for hh in range(nh):
            obuf[rows_p + n * length:rows_p + (n + 1) * length, hh * hd:(hh + 1) * hd] = (
                o[hh * length:(hh + 1) * length, hh * hd:(hh + 1) * hd])

    y = x + _rms(_bdot(obuf[...], w_o[...]), g_ref[3:4, :])
    yp_ref[...] = y[:rows_p]
    ys_ref[...] = y[rows_p:].reshape(nb, length, c)


def _attn(xp, xs, mem_k, mem_v, cache_k, cache_v, gains, i, w_q, w_o, length):
    b, s, c = xp.shape
    db, ls, _ = xs.shape
    nt = s // length
    nb = db // (b * nt)
    assert nb * b * nt == db
    lay = lambda bb, tt: (i, 0, 0)
    crow = mem_k.shape[2:]
    p_kv = pl.BlockSpec((None, None) + crow, lambda bb, tt: (i, bb, 0, 0))
    s_kv = pl.BlockSpec((None, nb) + crow, lambda bb, tt: (i, bb * nt + tt, 0, 0))
    xp_spec = pl.BlockSpec((None, length, c), lambda bb, tt: (bb, tt, 0))
    xs_spec = pl.BlockSpec((nb, ls, c), lambda bb, tt: (bb * nt + tt, 0, 0))
    return pl.pallas_call(
        _attn_kernel,
        grid=(b, nt),
        in_specs=[
            xp_spec, xs_spec, p_kv, p_kv, s_kv, s_kv,
            pl.BlockSpec((None, N_NORMS, c), lay),
            pl.BlockSpec((None, c, c), lay, pipeline_mode=pl.Buffered(1)),
            pl.BlockSpec((None, c, c), lay, pipeline_mode=pl.Buffered(1)),
        ],
        out_specs=[xp_spec, xs_spec],
        out_shape=[jax.ShapeDtypeStruct(xp.shape, F32), jax.ShapeDtypeStruct(xs.shape, F32)],
        scratch_shapes=[pltpu.VMEM((length + nb * ls, c), F32)],
        compiler_params=_params(),
        name="mem_cross_attn",
    )(xp, xs, mem_k, mem_v, cache_k, cache_v, gains, w_q, w_o)


def _ffn_kernel(sample, permute_out, *refs):
    x_ref, pre_ref, ins, outs, (act,), carry = _split_refs(refs, sample, 5, 2, 1)
    g_ref, w_up, w_dw, b_dw, w_down = ins
    y_ref, st_ref = outs
    t = pl.program_id(1)
    nt = pl.num_programs(1)
    if sample:
        (r8, n_cur), n_sub = x_ref.shape[:2], 1
    else:
        n_cur, r8, n_sub = PROMPT_SEG, SUBLANES, x_ref.shape[0] // PROMPT_TILE
    rows = n_cur * r8
    c = x_ref.shape[-1]
    cw = FFN_CHUNK
    hist = FFN_PREFIX

    x = _load_rows(x_ref, sample, False)
    hb = _rms(x, g_ref[4:5, :]).astype(BF16)

    if not sample:
        @pl.when(t == 0)
        def _():
            carry[...] = jnp.zeros(carry.shape, F32)

    def taps(u, past, cols):
        e = jnp.concatenate([past.reshape(hist * r8, cw), u], axis=0)
        y = b_dw[:, cols] + w_dw[FFN_CONV_WIDTH - 1:FFN_CONV_WIDTH, cols] * u
        for k in range(hist):
            y = y + w_dw[k:k + 1, cols] * e[k * r8:k * r8 + rows]
        return y

    def conv(cols):
        u = jnp.dot(hb, w_up[:, cols], preferred_element_type=F32)
        if sample:
            u3 = u.reshape(n_cur, r8, cw)
            for j in range(hist):
                st_ref[:, j, cols] = u3[n_cur - hist + j]
            return taps(u, jnp.stack([pre_ref[:, j, cols] for j in range(hist)]), cols)
        prev = carry[:, :, cols]
        ys = []
        for sub in range(n_sub):
            us = u[sub * rows:(sub + 1) * rows]
            last = us.reshape(n_cur, r8, cw)[n_cur - hist:]
            ys.append(taps(us, _segment_history(last, prev), cols))
            prev = last
        carry[:, :, cols] = prev
        return ys[0] if n_sub == 1 else jnp.concatenate(ys, axis=0)

    for ci in range(D_FF // cw):
        gcols = slice(ci * cw, (ci + 1) * cw)
        vcols = slice(D_FF + ci * cw, D_FF + (ci + 1) * cw)
        cg = conv(gcols)
        cv = conv(vcols)
        act[:, gcols] = (cg * _sigmoid(cg) * cv).astype(BF16)
    out = jnp.dot(act[...], w_down[...], preferred_element_type=F32)
    _store_rows(y_ref, x + _rms(out, g_ref[5:6, :]), sample, permute_out)

    if not sample:
        @pl.when(t == nt - 1)
        def _():
            st_ref[...] = carry[:, SUBLANES - 1, :]


def _ffn(x, prefix, acc, gains, i, w_up, w_dw, b_dw, w_down, nb, length, permute_out):
    b, s, c = x.shape
    f2 = 2 * D_FF
    sample = prefix is not None
    assert (s == length) if sample else (nb == 1)
    lay = lambda bb, tt: (i, 0, 0)
    single = pl.Buffered(1)
    in_specs = [_x_specs(sample, nb, length, c)]
    args = [x]
    if sample:
        in_specs.append(pl.BlockSpec((None, nb, FFN_PREFIX, f2), lambda bb, tt: (i, bb, 0, 0)))
        args.append(prefix)
    in_specs += [
        pl.BlockSpec((None, N_NORMS, c), lay),
        pl.BlockSpec((None, c, f2), lay, pipeline_mode=single),
        pl.BlockSpec((None, FFN_CONV_WIDTH, f2), lay),
        pl.BlockSpec((None, 1, f2), lay),
        pl.BlockSpec((None, D_FF, c), lay, pipeline_mode=single),
    ]
    args += [gains, w_up, w_dw, b_dw, w_down]
    mode, st_spec, st_shape, aliases = _state_io(prefix, acc, i, (nb, FFN_PREFIX, f2), lambda bb: (bb, 0, 0),
                                                 b, FFN_PREFIX, f2, in_specs, args)
    scratch = [pltpu.VMEM((nb * length, D_FF), BF16)]
    if not sample:
        scratch.append(pltpu.VMEM((FFN_PREFIX, SUBLANES, f2), F32))
    return pl.pallas_call(
        functools.partial(_ffn_kernel, mode, permute_out),
        grid=(b // nb, s // length),
        in_specs=in_specs,
        out_specs=[_x_specs(sample, nb, length, c), st_spec],
        out_shape=[jax.ShapeDtypeStruct((b, s, c), F32), st_shape],
        scratch_shapes=scratch,
        input_output_aliases=aliases,
        compiler_params=_params(),
        name="conv_ffn",
    )(*args)


def _mem_kv_kernel(n_cast, m_ref, g_ref, w_kv, *refs):
    src, (k_ref, v_ref), dst = refs[:n_cast], refs[n_cast:n_cast + 2], refs[n_cast + 2:]
    for s_ref, d_ref in zip(src, dst):
        d_ref[...] = s_ref[...].astype(BF16)
    nbm, n_mem, c = m_ref.shape
    kv = _bdot(_rms(m_ref[...].reshape(nbm * n_mem, c), g_ref[N_NORMS - 1:N_NORMS, :]), w_kv[...].astype(BF16))
    lane_tiles = MEM_HEAD_DIM // LANES
    stride = lane_tiles * N_MEM_HEADS
    for ref, base in ((k_ref, 0), (v_ref, c)):
        for bi in range(nbm):
            for hh in range(N_MEM_HEADS):
                for dt in range(lane_tiles):
                    col = base + hh * MEM_HEAD_DIM + dt * LANES
                    ref[bi, pl.ds(dt * N_MEM_HEADS + hh, n_mem, stride=stride), :] = (
                        kv[bi * n_mem:(bi + 1) * n_mem, col:col + LANES])


def _mem_kv(mem, gains, w_kv, to_cast, nbm):
    b, n_mem, c = mem.shape
    crows = n_mem * (c // LANES)
    steps = b // nbm
    out_spec = pl.BlockSpec((None, nbm, crows, LANES), lambda ii, rr: (ii, rr, 0, 0))
    out_shape = jax.ShapeDtypeStruct((DEPTH, b, crows, LANES), F32)
    slabs = [pl.BlockSpec((None, w.shape[1] // steps, w.shape[2]), lambda ii, rr: (ii, rr, 0)) for w in to_cast]
    return pl.pallas_call(
        functools.partial(_mem_kv_kernel, len(to_cast)),
        grid=(DEPTH, steps),
        in_specs=[
            pl.BlockSpec((nbm, n_mem, c), lambda ii, rr: (rr, 0, 0)),
            pl.BlockSpec((None, N_NORMS, c), lambda ii, rr: (ii, 0, 0)),
            pl.BlockSpec((None, c, 2 * c), lambda ii, rr: (ii, 0, 0), pipeline_mode=pl.Buffered(1)),
        ] + slabs,
        out_specs=[out_spec, out_spec] + slabs,
        out_shape=[out_shape, out_shape] + [jax.ShapeDtypeStruct(w.shape, BF16) for w in to_cast],
        compiler_params=_params(),
        name="mem_kv",
    )(mem, gains, w_kv, *to_cast)


_PLAN = {
    "prompt": dict(nb=1, length=PROMPT_TILE, ffn_length=2 * PROMPT_TILE, conf_rc=8),
    "sample": dict(nb=32, length=8, ffn_length=8, conf_rc=8),
}


def _mixer(group, x, i, pos0, pre, acc, p):
    plan = _PLAN[group]
    nb, length = plan["nb"], plan["length"]
    j = i // 2
    if i % 2 == 0:
        return _conformer(x, pre, acc, j, p["gains"], i, p["a_w_in"], p["a_b_in"], p["a_w_dw"], p["a_b_dw"],
                          p["a_ln_g"], p["a_ln_b"], p["a_w_out"], p["a_b_out"], nb, length, plan["conf_rc"],
                          permute_in=(group == "prompt" and i == 0))
    return _pool(x, pre, acc, j, pos0, p["gains"], i, p["p_w_group"], p["p_scale"], nb, length)


def _conv_ffn(group, x, i, ffn_pre, acc, p):
    plan = _PLAN[group]
    return _ffn(x, ffn_pre, acc, p["gains"], i, p["f_w_up"], p["f_w_dw"], p["f_b_dw"], p["f_w_down"], plan["nb"],
                plan["ffn_length"], permute_out=(group == "prompt" and i == DEPTH - 1))


def _cache_rows(cache):
    d, b = cache.shape[:2]
    lane_tiles = MEM_HEAD_DIM // LANES
    tiled = cache.reshape(d, b, N_MEM, N_MEM_HEADS, lane_tiles, LANES).transpose(0, 1, 2, 4, 3, 5)
    return tiled.reshape(d, b, N_MEM * lane_tiles * N_MEM_HEADS, LANES)


def _cache_unrows(rows):
    d, b = rows.shape[:2]
    lane_tiles = MEM_HEAD_DIM // LANES
    tiled = rows.reshape(d, b, N_MEM, lane_tiles, N_MEM_HEADS, LANES).transpose(0, 1, 2, 4, 3, 5)
    return tiled.reshape(d, b, N_MEM, N_MEM_HEADS, MEM_HEAD_DIM)


def kernel(x_prompt, x_sample, mem_prompt, state_conv, state_pool, state_ffn, cache_mem_k, cache_mem_v, norm_gains, a_w_in, a_b_in, a_w_dw, a_b_dw, a_ln_g, a_ln_b, a_w_out, a_b_out, p_w_group, p_scale, c_w_q, c_w_kv, c_w_o, f_w_up, f_w_dw, f_b_dw, f_w_down):
    b = x_prompt.shape[0]
    c = D_MODEL
    mk, mv, w_q_b, w_o_b, w_up_b, w_down_b = _mem_kv(mem_prompt, norm_gains, c_w_kv,
                                                     [c_w_q, c_w_o, f_w_up, f_w_down], 1)
    p = dict(
        gains=norm_gains,
        a_w_in=a_w_in.astype(BF16), a_b_in=a_b_in[:, None, :], a_w_dw=a_w_dw, a_b_dw=a_b_dw[:, None, :],
        a_ln_g=a_ln_g[:, None, :], a_ln_b=a_ln_b[:, None, :], a_w_out=a_w_out.astype(BF16),
        a_b_out=a_b_out[:, None, :],
        p_w_group=p_w_group.astype(BF16), p_scale=p_scale[:, None, :],
        c_w_q=w_q_b, c_w_o=w_o_b,
        f_w_up=w_up_b, f_w_dw=f_w_dw, f_b_dw=f_b_dw[:, None, :], f_w_down=w_down_b,
    )
    ck, cv = _cache_rows(cache_mem_k), _cache_rows(cache_mem_v)
    tm = lambda a: a.transpose(0, 2, 1, 3)
    conv_pre, pool_pre = tm(state_conv), tm(state_pool)
    xp, xs = x_prompt, x_sample
    mix_p, ffn_p = [], []
    mix_pre = (conv_pre, pool_pre)
    mix_s = [None, None]
    ffn_s = None
    for i in range(DEPTH):
        kind = i % 2
        xp, st = _mixer("prompt", xp, i, 0, None, None, p)
        mix_p.append(st)
        xs, mix_s[kind] = _mixer("sample", xs, i, PAST_LEN, mix_pre[kind], mix_s[kind], p)
        xp, xs = _attn(xp, xs, mk, mv, ck, cv, norm_gains, i, p["c_w_q"], p["c_w_o"], PROMPT_TILE)
        xp, st = _conv_ffn("prompt", xp, i, None, None, p)
        ffn_p.append(st)
        xs, ffn_s = _conv_ffn("sample", xs, i, state_ffn, ffn_s, p)
    return (xp, xs, jnp.stack(mix_p[0::2]), jnp.stack(mix_p[1::2]), jnp.stack(ffn_p),
            _cache_unrows(mk), _cache_unrows(mv), tm(mix_s[0]), tm(mix_s[1]), ffn_s)
```

```python
import functools

import jax
import jax.numpy as jnp
from jax import lax
from jax.experimental import pallas as pl
from jax.experimental.pallas import tpu as pltpu

D_MODEL = 1024
DEPTH = 4
PAST_LEN = 16384
CONV_WIDTH = 31
CONV_PREFIX = CONV_WIDTH - 1
POOL_WINDOWS = (2, 4, 8, 16)
POOL_GROUP_DIM = D_MODEL // len(POOL_WINDOWS)
POOL_PREFIX = max(POOL_WINDOWS) - 1
N_MEM = 256
N_MEM_HEADS = 4
MEM_HEAD_DIM = D_MODEL // N_MEM_HEADS
D_FF = 2816
FFN_CONV_WIDTH = 3
FFN_PREFIX = FFN_CONV_WIDTH - 1
N_NORMS = 7
RMS_EPS = 1e-6
LN_EPS = 1e-5

SUBLANES = 8
LANES = 128
PROMPT_TILE = 512
PROMPT_SEG = PROMPT_TILE // SUBLANES
FFN_CHUNK = 256
CONV_COLS = LANES
VMEM_LIMIT_BYTES = 56 * 1024 * 1024

BF16 = jnp.bfloat16
F32 = jnp.float32


def _rms(x, g):
    ms = jnp.mean(x * x, axis=-1, keepdims=True)
    return x * lax.rsqrt(ms + RMS_EPS) * g


def _bdot(a, w):
    return jnp.dot(a.astype(BF16), w, preferred_element_type=F32)


def _sigmoid(x):
    return 0.5 * jnp.tanh(0.5 * x) + 0.5


def _to_segments(x):
    rows, c = x.shape
    return x.reshape(SUBLANES, rows // SUBLANES, c).swapaxes(0, 1).reshape(rows, c)


def _from_segments(x):
    rows, c = x.shape
    tiles = [x[r:r + PROMPT_TILE].reshape(PROMPT_SEG, SUBLANES, c).swapaxes(0, 1).reshape(PROMPT_TILE, c)
             for r in range(0, rows, PROMPT_TILE)]
    return tiles[0] if len(tiles) == 1 else jnp.concatenate(tiles, axis=0)


def _load_rows(x_ref, sample, permute_in):
    if sample:
        return jnp.concatenate([x_ref[:, t, :] for t in range(x_ref.shape[1])], axis=0)
    x = x_ref[...]
    return _to_segments(x) if permute_in else x


def _store_rows(y_ref, y, sample, permute_out):
    if sample:
        nb = y_ref.shape[0]
        for t in range(y_ref.shape[1]):
            y_ref[:, t, :] = y[t * nb:(t + 1) * nb]
    else:
        y_ref[...] = _from_segments(y) if permute_out else y


def _segment_history(last, carry_val):
    sub = lax.broadcasted_iota(jnp.int32, last.shape, 1)
    return pltpu.roll(jnp.where(sub == SUBLANES - 1, carry_val, last), 1, axis=1)


def _fill_history(ext, cur3, hist, carry, pre_ref, t, sample):
    n_cur = cur3.shape[0]
    if sample:
        ext[0:hist] = pre_ref[...]
    else:
        @pl.when(t == 0)
        def _():
            carry[...] = jnp.zeros(carry.shape, F32)

        last = cur3[n_cur - hist:]
        ext[0:hist] = _segment_history(last, carry[...])
        carry[...] = last
    ext[hist:hist + n_cur] = cur3


def _store_state(st_ref, ext, hist, carry, t, nt, sample):
    n_cur = ext.shape[0] - hist
    if sample:
        st_ref[...] = ext[n_cur:n_cur + hist]
    else:
        @pl.when(t == nt - 1)
        def _():
            st_ref[...] = carry[:, SUBLANES - 1, :]


def _split_refs(refs, sample, n_in, n_out, n_scratch):
    refs = list(refs)
    x_ref = refs.pop(0)
    pre_ref = refs.pop(0) if sample else None
    ins, refs = refs[:n_in], refs[n_in:]
    if sample == "stacked":
        refs.pop(0)
    outs, refs = refs[:n_out], refs[n_out:]
    scr, refs = refs[:n_scratch], refs[n_scratch:]
    carry = None if sample else refs.pop(0)
    assert not refs
    return x_ref, pre_ref, ins, outs, scr, carry


def _conformer_kernel(sample, permute_in, rc, *refs):
    x_ref, pre_ref, ins, outs, scr, carry = _split_refs(refs, sample, 9, 2, 2)
    g_ref, w_in, b_in, w_dw, b_dw, ln_g, ln_b, w_out, b_out = ins
    y_ref, st_ref = outs
    ext, cbuf = scr
    t = pl.program_id(1)
    nt = pl.num_programs(1)
    n_cur, r8, c = cbuf.shape
    rows = n_cur * r8

    x = _load_rows(x_ref, sample, permute_in)
    h = _rms(x, g_ref[0:1, :])
    u = _bdot(h, w_in[...]) + b_in[...]
    glu = u[:, :c] * _sigmoid(u[:, c:])
    _fill_history(ext, glu.reshape(n_cur, r8, c), CONV_PREFIX, carry, pre_ref, t, sample)

    def chunk(ci, carry_):
        i0 = ci * rc
        for g in range(r8 // SUBLANES):
            rsl = slice(g * SUBLANES, (g + 1) * SUBLANES)
            def lane_col(l, carry2):
                cols = pl.ds(pl.multiple_of(l * CONV_COLS, CONV_COLS), CONV_COLS)
                bias = jnp.broadcast_to(b_dw[:, cols], (SUBLANES, CONV_COLS))
                accs = [bias] * rc
                for m in range(rc + CONV_PREFIX):
                    tile = ext[i0 + m, rsl, cols]
                    for j in range(max(0, m - CONV_PREFIX), min(rc, m + 1)):
                        accs[j] = accs[j] + w_dw[m - j:m - j + 1, cols] * tile
                cbuf[pl.ds(i0, rc), rsl, cols] = jnp.stack(accs)
                return carry2

            lax.fori_loop(0, c // CONV_COLS, lane_col, 0)
        return carry_

    lax.fori_loop(0, n_cur // rc, chunk, 0)

    v = cbuf[...].reshape(rows, c)
    mu = jnp.mean(v, axis=-1, keepdims=True)
    xc = v - mu
    var = jnp.mean(xc * xc, axis=-1, keepdims=True)
    yn = xc * lax.rsqrt(var + LN_EPS) * ln_g[...] + ln_b[...]
    out = _bdot(yn * _sigmoid(yn), w_out[...]) + b_out[...]
    _store_rows(y_ref, x + _rms(out, g_ref[1:2, :]), sample, False)
    _store_state(st_ref, ext, CONV_PREFIX, carry, t, nt, sample)


def _x_specs(sample, nb, length, c):
    if sample:
        return pl.BlockSpec((nb, length, c), lambda bb, tt: (bb, 0, 0))
    return pl.BlockSpec((None, length, c), lambda bb, tt: (bb, tt, 0))


def _state_io(prefix, acc, layer, block, index, b, hist, width, in_specs, args):
    if prefix is None:
        spec = pl.BlockSpec((None, hist, width), lambda bb, tt: (bb, 0, 0))
        return False, spec, jax.ShapeDtypeStruct((b, hist, width), F32), {}
    spec = pl.BlockSpec((None,) + block, lambda bb, tt: (layer,) + index(bb))
    shape = jax.ShapeDtypeStruct(prefix.shape, F32)
    in_specs.append(pl.BlockSpec(memory_space=pl.ANY))
    args.append(jnp.zeros(prefix.shape, F32) if acc is None else acc)
    return "stacked", spec, shape, {len(args) - 1: 1}


def _time_tiles(sample, nb, length):
    return (length, nb) if sample else (length // SUBLANES, SUBLANES)


def _params(sem=("arbitrary", "arbitrary")):
    return pltpu.CompilerParams(dimension_semantics=sem, vmem_limit_bytes=VMEM_LIMIT_BYTES)


def _conformer(x, prefix, acc, j, gains, i, w_in, b_in, w_dw, b_dw, ln_g, ln_b, w_out, b_out, nb, length, rc,
               permute_in):
    b, s, c = x.shape
    sample = prefix is not None
    assert (s == length) if sample else (nb == 1)
    n_cur, r8 = _time_tiles(sample, nb, length)
    lay = lambda bb, tt: (i, 0, 0)
    layj = lambda bb, tt: (j, 0, 0)
    single = pl.Buffered(1)
    in_specs = [_x_specs(sample, nb, length, c)]
    args = [x]
    if sample:
        in_specs.append(pl.BlockSpec((None, CONV_PREFIX, nb, c), lambda bb, tt: (j, 0, bb, 0)))
        args.append(prefix)
    in_specs += [
        pl.BlockSpec((None, N_NORMS, c), lay),
        pl.BlockSpec((None, c, 2 * c), layj, pipeline_mode=single),
        pl.BlockSpec((None, 1, 2 * c), layj),
        pl.BlockSpec((None, CONV_WIDTH, c), layj),
        pl.BlockSpec((None, 1, c), layj),
        pl.BlockSpec((None, 1, c), layj),
        pl.BlockSpec((None, 1, c), layj),
        pl.BlockSpec((None, c, c), layj, pipeline_mode=single),
        pl.BlockSpec((None, 1, c), layj),
    ]
    args += [gains, w_in, b_in, w_dw, b_dw, ln_g, ln_b, w_out, b_out]
    mode, st_spec, st_shape, aliases = _state_io(prefix, acc, j, (CONV_PREFIX, nb, c), lambda bb: (0, bb, 0),
                                                 b, CONV_PREFIX, c, in_specs, args)
    scratch = [pltpu.VMEM((CONV_PREFIX + n_cur, r8, c), F32), pltpu.VMEM((n_cur, r8, c), F32)]
    if not sample:
        scratch.append(pltpu.VMEM((CONV_PREFIX, SUBLANES, c), F32))
    return pl.pallas_call(
        functools.partial(_conformer_kernel, mode, permute_in, rc),
        grid=(b // nb, s // length),
        in_specs=in_specs,
        out_specs=[_x_specs(sample, nb, length, c), st_spec],
        out_shape=[jax.ShapeDtypeStruct((b, s, c), F32), st_shape],
        scratch_shapes=scratch,
        input_output_aliases=aliases,
        compiler_params=_params(),
        name="conformer_mixer",
    )(*args)


def _pool_kernel(sample, pos0, *refs):
    x_ref, pre_ref, ins, outs, _, carry = _split_refs(refs, sample, 3, 2, 0)
    g_ref, w_grp, scale = ins
    y_ref, st_ref = outs
    t = pl.program_id(1)
    nt = pl.num_programs(1)
    if sample:
        r8, n_cur = x_ref.shape[:2]
    else:
        n_cur, r8 = x_ref.shape[0] // SUBLANES, SUBLANES
    c = x_ref.shape[-1]
    rows = n_cur * r8
    gd = POOL_GROUP_DIM
    hist = POOL_PREFIX

    x = _load_rows(x_ref, sample, False)
    h3 = _rms(x, g_ref[0:1, :]).reshape(n_cur, r8, c)
    if sample:
        past = pre_ref[...]
    else:
        @pl.when(t == 0)
        def _():
            carry[...] = jnp.zeros(carry.shape, F32)

        last = h3[n_cur - hist:]
        past = _segment_history(last, carry[...])
        carry[...] = last
    e = jnp.concatenate([past, h3], axis=0)

    step = lax.broadcasted_iota(jnp.int32, (n_cur, r8, 1), 0)
    if sample:
        pos = pos0 + step
    else:
        pos = pos0 + t * rows + lax.broadcasted_iota(jnp.int32, (n_cur, r8, 1), 1) * n_cur + step
    posf = pos.astype(F32)

    outs_g = []
    for gi, w in enumerate(POOL_WINDOWS):
        eg = e[:, :, gi * gd:(gi + 1) * gd]
        s, m = eg, 1
        while m < w:
            s = s[m:] + s[:-m]
            m *= 2
        first = hist - (w - 1)
        inv = 1.0 / jnp.minimum(jnp.float32(w), posf + 1.0)
        pooled = s[first:first + n_cur] * inv - eg[hist:]
        outs_g.append(_bdot(pooled.reshape(rows, gd), w_grp[gi]))
    out = jnp.concatenate(outs_g, axis=-1) * scale[...]
    _store_rows(y_ref, x + _rms(out, g_ref[1:2, :]), sample, False)
    if sample:
        st_ref[...] = e[n_cur:]
    else:
        @pl.when(t == nt - 1)
        def _():
            st_ref[...] = carry[:, SUBLANES - 1, :]


def _pool(x, prefix, acc, j, pos0, gains, i, w_grp, scale, nb, length):
    b, s, c = x.shape
    sample = prefix is not None
    assert (s == length) if sample else (nb == 1)
    in_specs = [_x_specs(sample, nb, length, c)]
    args = [x]
    if sample:
        in_specs.append(pl.BlockSpec((None, POOL_PREFIX, nb, c), lambda bb, tt: (j, 0, bb, 0)))
        args.append(prefix)
    in_specs += [
        pl.BlockSpec((None, N_NORMS, c), lambda bb, tt: (i, 0, 0)),
        pl.BlockSpec((None, len(POOL_WINDOWS), POOL_GROUP_DIM, POOL_GROUP_DIM), lambda bb, tt: (j, 0, 0, 0)),
        pl.BlockSpec((None, 1, c), lambda bb, tt: (j, 0, 0)),
    ]
    args += [gains, w_grp, scale]
    mode, st_spec, st_shape, aliases = _state_io(prefix, acc, j, (POOL_PREFIX, nb, c), lambda bb: (0, bb, 0),
                                                 b, POOL_PREFIX, c, in_specs, args)
    scratch = [] if sample else [pltpu.VMEM((POOL_PREFIX, SUBLANES, c), F32)]
    return pl.pallas_call(
        functools.partial(_pool_kernel, mode, pos0),
        grid=(b // nb, s // length),
        in_specs=in_specs,
        out_specs=[_x_specs(sample, nb, length, c), st_spec],
        out_shape=[jax.ShapeDtypeStruct((b, s, c), F32), st_shape],
        scratch_shapes=scratch,
        input_output_aliases=aliases,
        compiler_params=_params(),
        name="pool_mixer",
    )(*args)


def _attend(qh, kh, vh):
    s = lax.dot_general(qh, kh.astype(BF16), (((1,), (1,)), ((), ())), preferred_element_type=F32)
    e = jnp.exp(s - jnp.max(s, axis=-1, keepdims=True))
    p = e * (1.0 / jnp.sum(e, axis=-1, keepdims=True))
    return jnp.dot(p.astype(BF16), vh.astype(BF16), preferred_element_type=F32)


def _cache_head(ref, hh):
    lane_tiles = MEM_HEAD_DIM // LANES
    stride = lane_tiles * N_MEM_HEADS
    parts = [ref[pl.ds(dt * N_MEM_HEADS + hh, N_MEM, stride=stride), :] for dt in range(lane_tiles)]
    return jnp.concatenate(parts, axis=-1)


def _attn_kernel(xp_ref, xs_ref, kp_ref, vp_ref, ks_ref, vs_ref, g_ref, w_q, w_o, yp_ref, ys_ref, obuf):
    rows_p, c = xp_ref.shape
    nb, length, _ = xs_ref.shape
    hd = MEM_HEAD_DIM
    nh = N_MEM_HEADS
    x = jnp.concatenate([xp_ref[...], xs_ref[...].reshape(nb * length, c)], axis=0)
    q = _bdot(_rms(x, g_ref[2:3, :]), w_q[...]) * (hd ** -0.5)

    qp = q[:rows_p].astype(BF16)
    for hh in range(nh):
        cols = slice(hh * hd, (hh + 1) * hd)
        obuf[0:rows_p, cols] = _attend(qp[:, cols], _cache_head(kp_ref, hh), _cache_head(vp_ref, hh))

    qs = q[rows_p:]
    col_head = lax.broadcasted_iota(jnp.int32, qs.shape, 1) // hd
    q_heads = [jnp.where(col_head == hh, qs, 0.0).astype(BF16) for hh in range(nh)]
    scores = []
    for n in range(nb):
        rsl = slice(n * length, (n + 1) * length)
        qn = jnp.concatenate([qh[rsl] for qh in q_heads], axis=0)
        kn = jnp.concatenate([_cache_head(ks_ref.at[n], hh) for hh in range(nh)], axis=-1).astype(BF16)
        scores.append(lax.dot_general(qn, kn, (((1,), (1,)), ((), ())), preferred_element_type=F32))
    s = jnp.concatenate(scores, axis=0)
    e = jnp.exp(s - jnp.max(s, axis=-1, keepdims=True))
    p = (e * (1.0 / jnp.sum(e, axis=-1, keepdims=True))).astype(BF16)
    for n in range(nb):
        vn = jnp.concatenate([_cache_head(vs_ref.at[n], hh) for hh in range(nh)], axis=-1).astype(BF16)
        o = jnp.dot(p[n * nh * length:(n + 1) * nh * length], vn, preferred_element_type=F32)
        for hh in range(nh):
            obuf[rows_p + n * length:rows_p + (n + 1) * length, hh * hd:(hh + 1) * hd] = (
                o[hh * length:(hh + 1) * length, hh * hd:(hh + 1) * hd])

    y = x + _rms(_bdot(obuf[...], w_o[...]), g_ref[3:4, :])
    yp_ref[...] = y[:rows_p]
    ys_ref[...] = y[rows_p:].reshape(nb, length, c)


def _attn(xp, xs, mem_k, mem_v, cache_k, cache_v, gains, i, w_q, w_o, length):
    b, s, c = xp.shape
    db, ls, _ = xs.shape
    nt = s // length
    nb = db // (b * nt)
    assert nb * b * nt == db
    lay = lambda bb, tt: (i, 0, 0)
    crow = mem_k.shape[2:]
    p_kv = pl.BlockSpec((None, None) + crow, lambda bb, tt: (i, bb, 0, 0))
    s_kv = pl.BlockSpec((None, nb) + crow, lambda bb, tt: (i, bb * nt + tt, 0, 0))
    xp_spec = pl.BlockSpec((None, length, c), lambda bb, tt: (bb, tt, 0))
    xs_spec = pl.BlockSpec((nb, ls, c), lambda bb, tt: (bb * nt + tt, 0, 0))
    return pl.pallas_call(
        _attn_kernel,
        grid=(b, nt),
        in_specs=[
            xp_spec, xs_spec, p_kv, p_kv, s_kv, s_kv,
            pl.BlockSpec((None, N_NORMS, c), lay),
            pl.BlockSpec((None, c, c), lay, pipeline_mode=pl.Buffered(1)),
            pl.BlockSpec((None, c, c), lay, pipeline_mode=pl.Buffered(1)),
        ],
        out_specs=[xp_spec, xs_spec],
        out_shape=[jax.ShapeDtypeStruct(xp.shape, F32), jax.ShapeDtypeStruct(xs.shape, F32)],
        scratch_shapes=[pltpu.VMEM((length + nb * ls, c), F32)],
        compiler_params=_params(),
        name="mem_cross_attn",
    )(xp, xs, mem_k, mem_v, cache_k, cache_v, gains, w_q, w_o)


def _ffn_kernel(sample, permute_out, *refs):
    x_ref, pre_ref, ins, outs, (act,), carry = _split_refs(refs, sample, 5, 2, 1)
    g_ref, w_up, w_dw, b_dw, w_down = ins
    y_ref, st_ref = outs
    t = pl.program_id(1)
    nt = pl.num_programs(1)
    if sample:
        (r8, n_cur), n_sub = x_ref.shape[:2], 1
    else:
        n_cur, r8, n_sub = PROMPT_SEG, SUBLANES, x_ref.shape[0] // PROMPT_TILE
    rows = n_cur * r8
    c = x_ref.shape[-1]
    cw = FFN_CHUNK
    hist = FFN_PREFIX

    x = _load_rows(x_ref, sample, False)
    hb = _rms(x, g_ref[4:5, :]).astype(BF16)

    if not sample:
        @pl.when(t == 0)
        def _():
            carry[...] = jnp.zeros(carry.shape, F32)

    def taps(u, past, cols):
        e = jnp.concatenate([past.reshape(hist * r8, cw), u], axis=0)
        y = b_dw[:, cols] + w_dw[FFN_CONV_WIDTH - 1:FFN_CONV_WIDTH, cols] * u
        for k in range(hist):
            y = y + w_dw[k:k + 1, cols] * e[k * r8:k * r8 + rows]
        return y

    def conv(cols):
        u = jnp.dot(hb, w_up[:, cols], preferred_element_type=F32)
        if sample:
            u3 = u.reshape(n_cur, r8, cw)
            for j in range(hist):
                st_ref[:, j, cols] = u3[n_cur - hist + j]
            return taps(u, jnp.stack([pre_ref[:, j, cols] for j in range(hist)]), cols)
        prev = carry[:, :, cols]
        ys = []
        for sub in range(n_sub):
            us = u[sub * rows:(sub + 1) * rows]
            last = us.reshape(n_cur, r8, cw)[n_cur - hist:]
            ys.append(taps(us, _segment_history(last, prev), cols))
            prev = last
        carry[:, :, cols] = prev
        return ys[0] if n_sub == 1 else jnp.concatenate(ys, axis=0)

    for ci in range(D_FF // cw):
        gcols = slice(ci * cw, (ci + 1) * cw)
        vcols = slice(D_FF + ci * cw, D_FF + (ci + 1) * cw)
        cg = conv(gcols)
        cv = conv(vcols)
        act[:, gcols] = (cg * _sigmoid(cg) * cv).astype(BF16)
    out = jnp.dot(act[...], w_down[...], preferred_element_type=F32)
    _store_rows(y_ref, x + _rms(out, g_ref[5:6, :]), sample, permute_out)

    if not sample:
        @pl.when(t == nt - 1)
        def _():
            st_ref[...] = carry[:, SUBLANES - 1, :]


def _ffn(x, prefix, acc, gains, i, w_up, w_dw, b_dw, w_down, nb, length, permute_out):
    b, s, c = x.shape
    f2 = 2 * D_FF
    sample = prefix is not None
    assert (s == length) if sample else (nb == 1)
    lay = lambda bb, tt: (i, 0, 0)
    single = pl.Buffered(1)
    in_specs = [_x_specs(sample, nb, length, c)]
    args = [x]
    if sample:
        in_specs.append(pl.BlockSpec((None, nb, FFN_PREFIX, f2), lambda bb, tt: (i, bb, 0, 0)))
        args.append(prefix)
    in_specs += [
        pl.BlockSpec((None, N_NORMS, c), lay),
        pl.BlockSpec((None, c, f2), lay, pipeline_mode=single),
        pl.BlockSpec((None, FFN_CONV_WIDTH, f2), lay),
        pl.BlockSpec((None, 1, f2), lay),
        pl.BlockSpec((None, D_FF, c), lay, pipeline_mode=single),
    ]
    args += [gains, w_up, w_dw, b_dw, w_down]
    mode, st_spec, st_shape, aliases = _state_io(prefix, acc, i, (nb, FFN_PREFIX, f2), lambda bb: (bb, 0, 0),
                                                 b, FFN_PREFIX, f2, in_specs, args)
    scratch = [pltpu.VMEM((nb * length, D_FF), BF16)]
    if not sample:
        scratch.append(pltpu.VMEM((FFN_PREFIX, SUBLANES, f2), F32))
    return pl.pallas_call(
        functools.partial(_ffn_kernel, mode, permute_out),
        grid=(b // nb, s // length),
        in_specs=in_specs,
        out_specs=[_x_specs(sample, nb, length, c), st_spec],
        out_shape=[jax.ShapeDtypeStruct((b, s, c), F32), st_shape],
        scratch_shapes=scratch,
        input_output_aliases=aliases,
        compiler_params=_params(),
        name="conv_ffn",
    )(*args)


def _mem_kv_kernel(n_cast, m_ref, g_ref, w_kv, *refs):
    src, (k_ref, v_ref), dst = refs[:n_cast], refs[n_cast:n_cast + 2], refs[n_cast + 2:]
    for s_ref, d_ref in zip(src, dst):
        d_ref[...] = s_ref[...].astype(BF16)
    nbm, n_mem, c = m_ref.shape
    kv = _bdot(_rms(m_ref[...].reshape(nbm * n_mem, c), g_ref[N_NORMS - 1:N_NORMS, :]), w_kv[...].astype(BF16))
    lane_tiles = MEM_HEAD_DIM // LANES
    stride = lane_tiles * N_MEM_HEADS
    for ref, base in ((k_ref, 0), (v_ref, c)):
        for bi in range(nbm):
            for hh in range(N_MEM_HEADS):
                for dt in range(lane_tiles):
                    col = base + hh * MEM_HEAD_DIM + dt * LANES
                    ref[bi, pl.ds(dt * N_MEM_HEADS + hh, n_mem, stride=stride), :] = (
                        kv[bi * n_mem:(bi + 1) * n_mem, col:col + LANES])


def _mem_kv(mem, gains, w_kv, to_cast, nbm):
    b, n_mem, c = mem.shape
    crows = n_mem * (c // LANES)
    steps = b // nbm
    out_spec = pl.BlockSpec((None, nbm, crows, LANES), lambda ii, rr: (ii, rr, 0, 0))
    out_shape = jax.ShapeDtypeStruct((DEPTH, b, crows, LANES), F32)
    slabs = [pl.BlockSpec((None, w.shape[1] // steps, w.shape[2]), lambda ii, rr: (ii, rr, 0)) for w in to_cast]
    return pl.pallas_call(
        functools.partial(_mem_kv_kernel, len(to_cast)),
        grid=(DEPTH, steps),
        in_specs=[
            pl.BlockSpec((nbm, n_mem, c), lambda ii, rr: (rr, 0, 0)),
            pl.BlockSpec((None, N_NORMS, c), lambda ii, rr: (ii, 0, 0)),
            pl.BlockSpec((None, c, 2 * c), lambda ii, rr: (ii, 0, 0), pipeline_mode=pl.Buffered(1)),
        ] + slabs,
        out_specs=[out_spec, out_spec] + slabs,
        out_shape=[out_shape, out_shape] + [jax.ShapeDtypeStruct(w.shape, BF16) for w in to_cast],
        compiler_params=_params(),
        name="mem_kv",
    )(mem, gains, w_kv, *to_cast)


_PLAN = {
    "prompt": dict(nb=1, length=PROMPT_TILE, ffn_length=2 * PROMPT_TILE, conf_rc=16),
    "sample": dict(nb=32, length=8, ffn_length=8, conf_rc=8),
}


def _mixer(group, x, i, pos0, pre, acc, p):
    plan = _PLAN[group]
    nb, length = plan["nb"], plan["length"]
    j = i // 2
    if i % 2 == 0:
        return _conformer(x, pre, acc, j, p["gains"], i, p["a_w_in"], p["a_b_in"], p["a_w_dw"], p["a_b_dw"],
                          p["a_ln_g"], p["a_ln_b"], p["a_w_out"], p["a_b_out"], nb, length, plan["conf_rc"],
                          permute_in=(group == "prompt" and i == 0))
    return _pool(x, pre, acc, j, pos0, p["gains"], i, p["p_w_group"], p["p_scale"], nb, length)


def _conv_ffn(group, x, i, ffn_pre, acc, p):
    plan = _PLAN[group]
    return _ffn(x, ffn_pre, acc, p["gains"], i, p["f_w_up"], p["f_w_dw"], p["f_b_dw"], p["f_w_down"], plan["nb"],
                plan["ffn_length"], permute_out=(group == "prompt" and i == DEPTH - 1))


def _cache_rows(cache):
    d, b = cache.shape[:2]
    lane_tiles = MEM_HEAD_DIM // LANES
    tiled = cache.reshape(d, b, N_MEM, N_MEM_HEADS, lane_tiles, LANES).transpose(0, 1, 2, 4, 3, 5)
    return tiled.reshape(d, b, N_MEM * lane_tiles * N_MEM_HEADS, LANES)


def _cache_unrows(rows):
    d, b = rows.shape[:2]
    lane_tiles = MEM_HEAD_DIM // LANES
    tiled = rows.reshape(d, b, N_MEM, lane_tiles, N_MEM_HEADS, LANES).transpose(0, 1, 2, 4, 3, 5)
    return tiled.reshape(d, b, N_MEM, N_MEM_HEADS, MEM_HEAD_DIM)


def kernel(x_prompt, x_sample, mem_prompt, state_conv, state_pool, state_ffn, cache_mem_k, cache_mem_v, norm_gains, a_w_in, a_b_in, a_w_dw, a_b_dw, a_ln_g, a_ln_b, a_w_out, a_b_out, p_w_group, p_scale, c_w_q, c_w_kv, c_w_o, f_w_up, f_w_dw, f_b_dw, f_w_down):
    b = x_prompt.shape[0]
    c = D_MODEL
    mk, mv, w_q_b, w_o_b, w_up_b, w_down_b = _mem_kv(mem_prompt, norm_gains, c_w_kv,
                                                     [c_w_q, c_w_o, f_w_up, f_w_down], 1)
    p = dict(
        gains=norm_gains,
        a_w_in=a_w_in.astype(BF16), a_b_in=a_b_in[:, None, :], a_w_dw=a_w_dw, a_b_dw=a_b_dw[:, None, :],
        a_ln_g=a_ln_g[:, None, :], a_ln_b=a_ln_b[:, None, :], a_w_out=a_w_out.astype(BF16),
        a_b_out=a_b_out[:, None, :],
        p_w_group=p_w_group.astype(BF16), p_scale=p_scale[:, None, :],
        c_w_q=w_q_b, c_w_o=w_o_b,
        f_w_up=w_up_b, f_w_dw=f_w_dw, f_b_dw=f_b_dw[:, None, :], f_w_down=w_down_b,
    )
    ck, cv = _cache_rows(cache_mem_k), _cache_rows(cache_mem_v)
    tm = lambda a: a.transpose(0, 2, 1, 3)
    conv_pre, pool_pre = tm(state_conv), tm(state_pool)
    xp, xs = x_prompt, x_sample
    mix_p, ffn_p = [], []
    mix_pre = (conv_pre, pool_pre)
    mix_s = [None, None]
    ffn_s = None
    for i in range(DEPTH):
        kind = i % 2
        xp, st = _mixer("prompt", xp, i, 0, None, None, p)
        mix_p.append(st)
        xs, mix_s[kind] = _mixer("sample", xs, i, PAST_LEN, mix_pre[kind], mix_s[kind], p)
        xp, xs = _attn(xp, xs, mk, mv, ck, cv, norm_gains, i, p["c_w_q"], p["c_w_o"], PROMPT_TILE)
        xp, st = _conv_ffn("prompt", xp, i, None, None, p)
        ffn_p.append(st)
        xs, ffn_s = _conv_ffn("sample", xs, i, state_ffn, ffn_s, p)
    return (xp, xs, jnp.stack(mix_p[0::2]), jnp.stack(mix_p[1::2]), jnp.stack(ffn_p),
            _cache_unrows(mk), _cache_unrows(mv), tm(mix_s[0]), tm(mix_s[1]), ffn_s)
```

```python
import functools

import jax
import jax.numpy as jnp
from jax import lax
from jax.experimental import pallas as pl
from jax.experimental.pallas import tpu as pltpu

D_MODEL = 1024
DEPTH = 4
PAST_LEN = 16384
CONV_WIDTH = 31
CONV_PREFIX = CONV_WIDTH - 1
POOL_WINDOWS = (2, 4, 8, 16)
POOL_GROUP_DIM = D_MODEL // len(POOL_WINDOWS)
POOL_PREFIX = max(POOL_WINDOWS) - 1
N_MEM = 256
N_MEM_HEADS = 4
MEM_HEAD_DIM = D_MODEL // N_MEM_HEADS
D_FF = 2816
FFN_CONV_WIDTH = 3
FFN_PREFIX = FFN_CONV_WIDTH - 1
N_NORMS = 7
RMS_EPS = 1e-6
LN_EPS = 1e-5

SUBLANES = 8
LANES = 128
PROMPT_TILE = 512
PROMPT_SEG = PROMPT_TILE // SUBLANES
FFN_CHUNK = 256
CONV_COLS = LANES
VMEM_LIMIT_BYTES = 56 * 1024 * 1024

BF16 = jnp.bfloat16
F32 = jnp.float32


def _rms(x, g):
    ms = jnp.mean(x * x, axis=-1, keepdims=True)
    return x * lax.rsqrt(ms + RMS_EPS) * g


def _bdot(a, w):
    return jnp.dot(a.astype(BF16), w, preferred_element_type=F32)


def _sigmoid(x):
    return 0.5 * jnp.tanh(0.5 * x) + 0.5


def _to_segments(x):
    rows, c = x.shape
    return x.reshape(SUBLANES, rows // SUBLANES, c).swapaxes(0, 1).reshape(rows, c)


def _from_segments(x):
    rows, c = x.shape
    tiles = [x[r:r + PROMPT_TILE].reshape(PROMPT_SEG, SUBLANES, c).swapaxes(0, 1).reshape(PROMPT_TILE, c)
             for r in range(0, rows, PROMPT_TILE)]
    return tiles[0] if len(tiles) == 1 else jnp.concatenate(tiles, axis=0)


def _load_rows(x_ref, sample, permute_in):
    if sample:
        return jnp.concatenate([x_ref[:, t, :] for t in range(x_ref.shape[1])], axis=0)
    x = x_ref[...]
    return _to_segments(x) if permute_in else x


def _store_rows(y_ref, y, sample, permute_out):
    if sample:
        nb = y_ref.shape[0]
        for t in range(y_ref.shape[1]):
            y_ref[:, t, :] = y[t * nb:(t + 1) * nb]
    else:
        y_ref[...] = _from_segments(y) if permute_out else y


def _segment_history(last, carry_val):
    sub = lax.broadcasted_iota(jnp.int32, last.shape, 1)
    return pltpu.roll(jnp.where(sub == SUBLANES - 1, carry_val, last), 1, axis=1)


def _fill_history(ext, cur3, hist, carry, pre_ref, t, sample):
    n_cur = cur3.shape[0]
    if sample:
        ext[0:hist] = pre_ref[...]
    else:
        @pl.when(t == 0)
        def _():
            carry[...] = jnp.zeros(carry.shape, F32)

        last = cur3[n_cur - hist:]
        ext[0:hist] = _segment_history(last, carry[...])
        carry[...] = last
    ext[hist:hist + n_cur] = cur3


def _store_state(st_ref, ext, hist, carry, t, nt, sample):
    n_cur = ext.shape[0] - hist
    if sample:
        st_ref[...] = ext[n_cur:n_cur + hist]
    else:
        @pl.when(t == nt - 1)
        def _():
            st_ref[...] = carry[:, SUBLANES - 1, :]


def _split_refs(refs, sample, n_in, n_out, n_scratch):
    refs = list(refs)
    x_ref = refs.pop(0)
    pre_ref = refs.pop(0) if sample else None
    ins, refs = refs[:n_in], refs[n_in:]
    if sample == "stacked":
        refs.pop(0)
    outs, refs = refs[:n_out], refs[n_out:]
    scr, refs = refs[:n_scratch], refs[n_scratch:]
    carry = None if sample else refs.pop(0)
    assert not refs
    return x_ref, pre_ref, ins, outs, scr, carry


def _conformer_kernel(sample, permute_in, rc, *refs):
    x_ref, pre_ref, ins, outs, scr, carry = _split_refs(refs, sample, 9, 2, 2)
    g_ref, w_in, b_in, w_dw, b_dw, ln_g, ln_b, w_out, b_out = ins
    y_ref, st_ref = outs
    ext, cbuf = scr
    t = pl.program_id(1)
    nt = pl.num_programs(1)
    n_cur, r8, c = cbuf.shape
    rows = n_cur * r8

    x = _load_rows(x_ref, sample, permute_in)
    h = _rms(x, g_ref[0:1, :])
    u = _bdot(h, w_in[...]) + b_in[...]
    glu = u[:, :c] * _sigmoid(u[:, c:])
    _fill_history(ext, glu.reshape(n_cur, r8, c), CONV_PREFIX, carry, pre_ref, t, sample)

    def chunk(ci, carry_):
        i0 = ci * rc
        for g in range(r8 // SUBLANES):
            rsl = slice(g * SUBLANES, (g + 1) * SUBLANES)
            def lane_col(l, carry2):
                cols = pl.ds(pl.multiple_of(l * CONV_COLS, CONV_COLS), CONV_COLS)
                bias = jnp.broadcast_to(b_dw[:, cols], (SUBLANES, CONV_COLS))
                accs = [bias] * rc
                for m in range(rc + CONV_PREFIX):
                    tile = ext[i0 + m, rsl, cols]
                    for j in range(max(0, m - CONV_PREFIX), min(rc, m + 1)):
                        accs[j] = accs[j] + w_dw[m - j:m - j + 1, cols] * tile
                cbuf[pl.ds(i0, rc), rsl, cols] = jnp.stack(accs)
                return carry2

            lax.fori_loop(0, c // CONV_COLS, lane_col, 0)
        return carry_

    lax.fori_loop(0, n_cur // rc, chunk, 0)

    v = cbuf[...].reshape(rows, c)
    mu = jnp.mean(v, axis=-1, keepdims=True)
    xc = v - mu
    var = jnp.mean(xc * xc, axis=-1, keepdims=True)
    yn = xc * lax.rsqrt(var + LN_EPS) * ln_g[...] + ln_b[...]
    out = _bdot(yn * _sigmoid(yn), w_out[...]) + b_out[...]
    _store_rows(y_ref, x + _rms(out, g_ref[1:2, :]), sample, False)
    _store_state(st_ref, ext, CONV_PREFIX, carry, t, nt, sample)


def _x_specs(sample, nb, length, c):
    if sample:
        return pl.BlockSpec((nb, length, c), lambda bb, tt: (bb, 0, 0))
    return pl.BlockSpec((None, length, c), lambda bb, tt: (bb, tt, 0))


def _state_io(prefix, acc, layer, block, index, b, hist, width, in_specs, args):
    if prefix is None:
        spec = pl.BlockSpec((None, hist, width), lambda bb, tt: (bb, 0, 0))
        return False, spec, jax.ShapeDtypeStruct((b, hist, width), F32), {}
    spec = pl.BlockSpec((None,) + block, lambda bb, tt: (layer,) + index(bb))
    shape = jax.ShapeDtypeStruct(prefix.shape, F32)
    in_specs.append(pl.BlockSpec(memory_space=pl.ANY))
    args.append(jnp.zeros(prefix.shape, F32) if acc is None else acc)
    return "stacked", spec, shape, {len(args) - 1: 1}


def _time_tiles(sample, nb, length):
    return (length, nb) if sample else (length // SUBLANES, SUBLANES)


def _params(sem=("arbitrary", "arbitrary")):
    return pltpu.CompilerParams(dimension_semantics=sem, vmem_limit_bytes=VMEM_LIMIT_BYTES)


def _conformer(x, prefix, acc, j, gains, i, w_in, b_in, w_dw, b_dw, ln_g, ln_b, w_out, b_out, nb, length, rc,
               permute_in):
    b, s, c = x.shape
    sample = prefix is not None
    assert (s == length) if sample else (nb == 1)
    n_cur, r8 = _time_tiles(sample, nb, length)
    lay = lambda bb, tt: (i, 0, 0)
    layj = lambda bb, tt: (j, 0, 0)
    single = pl.Buffered(1)
    in_specs = [_x_specs(sample, nb, length, c)]
    args = [x]
    if sample:
        in_specs.append(pl.BlockSpec((None, CONV_PREFIX, nb, c), lambda bb, tt: (j, 0, bb, 0)))
        args.append(prefix)
    in_specs += [
        pl.BlockSpec((None, N_NORMS, c), lay),
        pl.BlockSpec((None, c, 2 * c), layj, pipeline_mode=single),
        pl.BlockSpec((None, 1, 2 * c), layj),
        pl.BlockSpec((None, CONV_WIDTH, c), layj),
        pl.BlockSpec((None, 1, c), layj),
        pl.BlockSpec((None, 1, c), layj),
        pl.BlockSpec((None, 1, c), layj),
        pl.BlockSpec((None, c, c), layj, pipeline_mode=single),
        pl.BlockSpec((None, 1, c), layj),
    ]
    args += [gains, w_in, b_in, w_dw, b_dw, ln_g, ln_b, w_out, b_out]
    mode, st_spec, st_shape, aliases = _state_io(prefix, acc, j, (CONV_PREFIX, nb, c), lambda bb: (0, bb, 0),
                                                 b, CONV_PREFIX, c, in_specs, args)
    scratch = [pltpu.VMEM((CONV_PREFIX + n_cur, r8, c), F32), pltpu.VMEM((n_cur, r8, c), F32)]
    if not sample:
        scratch.append(pltpu.VMEM((CONV_PREFIX, SUBLANES, c), F32))
    return pl.pallas_call(
        functools.partial(_conformer_kernel, mode, permute_in, rc),
        grid=(b // nb, s // length),
        in_specs=in_specs,
        out_specs=[_x_specs(sample, nb, length, c), st_spec],
        out_shape=[jax.ShapeDtypeStruct((b, s, c), F32), st_shape],
        scratch_shapes=scratch,
        input_output_aliases=aliases,
        compiler_params=_params(),
        name="conformer_mixer",
    )(*args)


def _pool_kernel(sample, pos0, *refs):
    x_ref, pre_ref, ins, outs, _, carry = _split_refs(refs, sample, 3, 2, 0)
    g_ref, w_grp, scale = ins
    y_ref, st_ref = outs
    t = pl.program_id(1)
    nt = pl.num_programs(1)
    if sample:
        r8, n_cur = x_ref.shape[:2]
    else:
        n_cur, r8 = x_ref.shape[0] // SUBLANES, SUBLANES
    c = x_ref.shape[-1]
    rows = n_cur * r8
    gd = POOL_GROUP_DIM
    hist = POOL_PREFIX

    x = _load_rows(x_ref, sample, False)
    h3 = _rms(x, g_ref[0:1, :]).reshape(n_cur, r8, c)
    if sample:
        past = pre_ref[...]
    else:
        @pl.when(t == 0)
        def _():
            carry[...] = jnp.zeros(carry.shape, F32)

        last = h3[n_cur - hist:]
        past = _segment_history(last, carry[...])
        carry[...] = last
    e = jnp.concatenate([past, h3], axis=0)

    step = lax.broadcasted_iota(jnp.int32, (n_cur, r8, 1), 0)
    if sample:
        pos = pos0 + step
    else:
        pos = pos0 + t * rows + lax.broadcasted_iota(jnp.int32, (n_cur, r8, 1), 1) * n_cur + step
    posf = pos.astype(F32)

    outs_g = []
    for gi, w in enumerate(POOL_WINDOWS):
        eg = e[:, :, gi * gd:(gi + 1) * gd]
        s, m = eg, 1
        while m < w:
            s = s[m:] + s[:-m]
            m *= 2
        first = hist - (w - 1)
        inv = 1.0 / jnp.minimum(jnp.float32(w), posf + 1.0)
        pooled = s[first:first + n_cur] * inv - eg[hist:]
        outs_g.append(_bdot(pooled.reshape(rows, gd), w_grp[gi]))
    out = jnp.concatenate(outs_g, axis=-1) * scale[...]
    _store_rows(y_ref, x + _rms(out, g_ref[1:2, :]), sample, False)
    if sample:
        st_ref[...] = e[n_cur:]
    else:
        @pl.when(t == nt - 1)
        def _():
            st_ref[...] = carry[:, SUBLANES - 1, :]


def _pool(x, prefix, acc, j, pos0, gains, i, w_grp, scale, nb, length):
    b, s, c = x.shape
    sample = prefix is not None
    assert (s == length) if sample else (nb == 1)
    in_specs = [_x_specs(sample, nb, length, c)]
    args = [x]
    if sample:
        in_specs.append(pl.BlockSpec((None, POOL_PREFIX, nb, c), lambda bb, tt: (j, 0, bb, 0)))
        args.append(prefix)
    in_specs += [
        pl.BlockSpec((None, N_NORMS, c), lambda bb, tt: (i, 0, 0)),
        pl.BlockSpec((None, len(POOL_WINDOWS), POOL_GROUP_DIM, POOL_GROUP_DIM), lambda bb, tt: (j, 0, 0, 0)),
        pl.BlockSpec((None, 1, c), lambda bb, tt: (j, 0, 0)),
    ]
    args += [gains, w_grp, scale]
    mode, st_spec, st_shape, aliases = _state_io(prefix, acc, j, (POOL_PREFIX, nb, c), lambda bb: (0, bb, 0),
                                                 b, POOL_PREFIX, c, in_specs, args)
    scratch = [] if sample else [pltpu.VMEM((POOL_PREFIX, SUBLANES, c), F32)]
    return pl.pallas_call(
        functools.partial(_pool_kernel, mode, pos0),
        grid=(b // nb, s // length),
        in_specs=in_specs,
        out_specs=[_x_specs(sample, nb, length, c), st_spec],
        out_shape=[jax.ShapeDtypeStruct((b, s, c), F32), st_shape],
        scratch_shapes=scratch,
        input_output_aliases=aliases,
        compiler_params=_params(),
        name="pool_mixer",
    )(*args)


def _attend(qh, kh, vh):
    s = lax.dot_general(qh, kh.astype(BF16), (((1,), (1,)), ((), ())), preferred_element_type=F32)
    e = jnp.exp(s - jnp.max(s, axis=-1, keepdims=True))
    p = e * (1.0 / jnp.sum(e, axis=-1, keepdims=True))
    return jnp.dot(p.astype(BF16), vh.astype(BF16), preferred_element_type=F32)


def _cache_head(ref, hh):
    lane_tiles = MEM_HEAD_DIM // LANES
    stride = lane_tiles * N_MEM_HEADS
    parts = [ref[pl.ds(dt * N_MEM_HEADS + hh, N_MEM, stride=stride), :] for dt in range(lane_tiles)]
    return jnp.concatenate(parts, axis=-1)


def _attn_kernel(xp_ref, xs_ref, kp_ref, vp_ref, ks_ref, vs_ref, g_ref, w_q, w_o, yp_ref, ys_ref, obuf):
    rows_p, c = xp_ref.shape
    nb, length, _ = xs_ref.shape
    hd = MEM_HEAD_DIM
    nh = N_MEM_HEADS
    x = jnp.concatenate([xp_ref[...], xs_ref[...].reshape(nb * length, c)], axis=0)
    q = _bdot(_rms(x, g_ref[2:3, :]), w_q[...]) * (hd ** -0.5)

    qp = q[:rows_p].astype(BF16)
    for hh in range(nh):
        cols = slice(hh * hd, (hh + 1) * hd)
        obuf[0:rows_p, cols] = _attend(qp[:, cols], _cache_head(kp_ref, hh), _cache_head(vp_ref, hh))

    qs = q[rows_p:]
    col_head = lax.broadcasted_iota(jnp.int32, qs.shape, 1) // hd
    q_heads = [jnp.where(col_head == hh, qs, 0.0).astype(BF16) for hh in range(nh)]
    scores = []
    for n in range(nb):
        rsl = slice(n * length, (n + 1) * length)
        qn = jnp.concatenate([qh[rsl] for qh in q_heads], axis=0)
        kn = jnp.concatenate([_cache_head(ks_ref.at[n], hh) for hh in range(nh)], axis=-1).astype(BF16)
        scores.append(lax.dot_general(qn, kn, (((1,), (1,)), ((), ())), preferred_element_type=F32))
    s = jnp.concatenate(scores, axis=0)
    e = jnp.exp(s - jnp.max(s, axis=-1, keepdims=True))
    p = (e * (1.0 / jnp.sum(e, axis=-1, keepdims=True))).astype(BF16)
    for n in range(nb):
        vn = jnp.concatenate([_cache_head(vs_ref.at[n], hh) for hh in range(nh)], axis=-1).astype(BF16)
        o = jnp.dot(p[n * nh * length:(n + 1) * nh * length], vn, preferred_element_type=F32)
        for hh in range(nh):
            obuf[rows_p + n * length:rows_p + (n + 1) * length, hh * hd:(hh + 1) * hd] = (
                o[hh * length:(hh + 1) * length, hh * hd:(hh + 1) * hd])

    y = x + _rms(_bdot(obuf[...], w_o[...]), g_ref[3:4, :])
    yp_ref[...] = y[:rows_p]
    ys_ref[...] = y[rows_p:].reshape(nb, length, c)


def _attn(xp, xs, mem_k, mem_v, cache_k, cache_v, gains, i, w_q, w_o, length):
    b, s, c = xp.shape
    db, ls, _ = xs.shape
    nt = s // length
    nb = db // (b * nt)
    assert nb * b * nt == db
    lay = lambda bb, tt: (i, 0, 0)
    crow = mem_k.shape[2:]
    p_kv = pl.BlockSpec((None, None) + crow, lambda bb, tt: (i, bb, 0, 0))
    s_kv = pl.BlockSpec((None, nb) + crow, lambda bb, tt: (i, bb * nt + tt, 0, 0))
    xp_spec = pl.BlockSpec((None, length, c), lambda bb, tt: (bb, tt, 0))
    xs_spec = pl.BlockSpec((nb, ls, c), lambda bb, tt: (bb * nt + tt, 0, 0))
    return pl.pallas_call(
        _attn_kernel,
        grid=(b, nt),
        in_specs=[
            xp_spec, xs_spec, p_kv, p_kv, s_kv, s_kv,
            pl.BlockSpec((None, N_NORMS, c), lay),
            pl.BlockSpec((None, c, c), lay, pipeline_mode=pl.Buffered(1)),
            pl.BlockSpec((None, c, c), lay, pipeline_mode=pl.Buffered(1)),
        ],
        out_specs=[xp_spec, xs_spec],
        out_shape=[jax.ShapeDtypeStruct(xp.shape, F32), jax.ShapeDtypeStruct(xs.shape, F32)],
        scratch_shapes=[pltpu.VMEM((length + nb * ls, c), F32)],
        compiler_params=_params(),
        name="mem_cross_attn",
    )(xp, xs, mem_k, mem_v, cache_k, cache_v, gains, w_q, w_o)


def _ffn_kernel(sample, permute_out, *refs):
    x_ref, pre_ref, ins, outs, (act,), carry = _split_refs(refs, sample, 5, 2, 1)
    g_ref, w_up, w_dw, b_dw, w_down = ins
    y_ref, st_ref = outs
    t = pl.program_id(1)
    nt = pl.num_programs(1)
    if sample:
        (r8, n_cur), n_sub = x_ref.shape[:2], 1
    else:
        n_cur, r8, n_sub = PROMPT_SEG, SUBLANES, x_ref.shape[0] // PROMPT_TILE
    rows = n_cur * r8
    c = x_ref.shape[-1]
    cw = FFN_CHUNK
    hist = FFN_PREFIX

    x = _load_rows(x_ref, sample, False)
    hb = _rms(x, g_ref[4:5, :]).astype(BF16)

    if not sample:
        @pl.when(t == 0)
        def _():
            carry[...] = jnp.zeros(carry.shape, F32)

    def taps(u, past, cols):
        e = jnp.concatenate([past.reshape(hist * r8, cw), u], axis=0)
        y = b_dw[:, cols] + w_dw[FFN_CONV_WIDTH - 1:FFN_CONV_WIDTH, cols] * u
        for k in range(hist):
            y = y + w_dw[k:k + 1, cols] * e[k * r8:k * r8 + rows]
        return y

    def conv(cols):
        u = jnp.dot(hb, w_up[:, cols], preferred_element_type=F32)
        if sample:
            u3 = u.reshape(n_cur, r8, cw)
            for j in range(hist):
                st_ref[:, j, cols] = u3[n_cur - hist + j]
            return taps(u, jnp.stack([pre_ref[:, j, cols] for j in range(hist)]), cols)
        prev = carry[:, :, cols]
        ys = []
        for sub in range(n_sub):
            us = u[sub * rows:(sub + 1) * rows]
            last = us.reshape(n_cur, r8, cw)[n_cur - hist:]
            ys.append(taps(us, _segment_history(last, prev), cols))
            prev = last
        carry[:, :, cols] = prev
        return ys[0] if n_sub == 1 else jnp.concatenate(ys, axis=0)

    for ci in range(D_FF // cw):
        gcols = slice(ci * cw, (ci + 1) * cw)
        vcols = slice(D_FF + ci * cw, D_FF + (ci + 1) * cw)
        cg = conv(gcols)
        cv = conv(vcols)
        act[:, gcols] = (cg * _sigmoid(cg) * cv).astype(BF16)
    out = jnp.dot(act[...], w_down[...], preferred_element_type=F32)
    _store_rows(y_ref, x + _rms(out, g_ref[5:6, :]), sample, permute_out)

    if not sample:
        @pl.when(t == nt - 1)
        def _():
            st_ref[...] = carry[:, SUBLANES - 1, :]


def _ffn(x, prefix, acc, gains, i, w_up, w_dw, b_dw, w_down, nb, length, permute_out):
    b, s, c = x.shape
    f2 = 2 * D_FF
    sample = prefix is not None
    assert (s == length) if sample else (nb == 1)
    lay = lambda bb, tt: (i, 0, 0)
    single = pl.Buffered(1)
    in_specs = [_x_specs(sample, nb, length, c)]
    args = [x]
    if sample:
        in_specs.append(pl.BlockSpec((None, nb, FFN_PREFIX, f2), lambda bb, tt: (i, bb, 0, 0)))
        args.append(prefix)
    in_specs += [
        pl.BlockSpec((None, N_NORMS, c), lay),
        pl.BlockSpec((None, c, f2), lay, pipeline_mode=single),
        pl.BlockSpec((None, FFN_CONV_WIDTH, f2), lay),
        pl.BlockSpec((None, 1, f2), lay),
        pl.BlockSpec((None, D_FF, c), lay, pipeline_mode=single),
    ]
    args += [gains, w_up, w_dw, b_dw, w_down]
    mode, st_spec, st_shape, aliases = _state_io(prefix, acc, i, (nb, FFN_PREFIX, f2), lambda bb: (bb, 0, 0),
                                                 b, FFN_PREFIX, f2, in_specs, args)
    scratch = [pltpu.VMEM((nb * length, D_FF), BF16)]
    if not sample:
        scratch.append(pltpu.VMEM((FFN_PREFIX, SUBLANES, f2), F32))
    return pl.pallas_call(
        functools.partial(_ffn_kernel, mode, permute_out),
        grid=(b // nb, s // length),
        in_specs=in_specs,
        out_specs=[_x_specs(sample, nb, length, c), st_spec],
        out_shape=[jax.ShapeDtypeStruct((b, s, c), F32), st_shape],
        scratch_shapes=scratch,
        input_output_aliases=aliases,
        compiler_params=_params(),
        name="conv_ffn",
    )(*args)


def _mem_kv_kernel(n_cast, m_ref, g_ref, w_kv, *refs):
    src, (k_ref, v_ref), dst = refs[:n_cast], refs[n_cast:n_cast + 2], refs[n_cast + 2:]
    for s_ref, d_ref in zip(src, dst):
        d_ref[...] = s_ref[...].astype(BF16)
    nbm, n_mem, c = m_ref.shape
    kv = _bdot(_rms(m_ref[...].reshape(nbm * n_mem, c), g_ref[N_NORMS - 1:N_NORMS, :]), w_kv[...].astype(BF16))
    lane_tiles = MEM_HEAD_DIM // LANES
    stride = lane_tiles * N_MEM_HEADS
    for ref, base in ((k_ref, 0), (v_ref, c)):
        for bi in range(nbm):
            for hh in range(N_MEM_HEADS):
                for dt in range(lane_tiles):
                    col = base + hh * MEM_HEAD_DIM + dt * LANES
                    ref[bi, pl.ds(dt * N_MEM_HEADS + hh, n_mem, stride=stride), :] = (
                        kv[bi * n_mem:(bi + 1) * n_mem, col:col + LANES])


def _mem_kv(mem, gains, w_kv, to_cast, nbm):
    b, n_mem, c = mem.shape
    crows = n_mem * (c // LANES)
    steps = b // nbm
    out_spec = pl.BlockSpec((None, nbm, crows, LANES), lambda ii, rr: (ii, rr, 0, 0))
    out_shape = jax.ShapeDtypeStruct((DEPTH, b, crows, LANES), F32)
    slabs = [pl.BlockSpec((None, w.shape[1] // steps, w.shape[2]), lambda ii, rr: (ii, rr, 0)) for w in to_cast]
    return pl.pallas_call(
        functools.partial(_mem_kv_kernel, len(to_cast)),
        grid=(DEPTH, steps),
        in_specs=[
            pl.BlockSpec((nbm, n_mem, c), lambda ii, rr: (rr, 0, 0)),
            pl.BlockSpec((None, N_NORMS, c), lambda ii, rr: (ii, 0, 0)),
            pl.BlockSpec((None, c, 2 * c), lambda ii, rr: (ii, 0, 0), pipeline_mode=pl.Buffered(1)),
        ] + slabs,
        out_specs=[out_spec, out_spec] + slabs,
        out_shape=[out_shape, out_shape] + [jax.ShapeDtypeStruct(w.shape, BF16) for w in to_cast],
        compiler_params=_params(),
        name="mem_kv",
    )(mem, gains, w_kv, *to_cast)


_PLAN = {
    "prompt": dict(nb=1, length=PROMPT_TILE, ffn_length=2 * PROMPT_TILE, conf_rc=32),
    "sample": dict(nb=32, length=8, ffn_length=8, conf_rc=8),
}


def _mixer(group, x, i, pos0, pre, acc, p):
    plan = _PLAN[group]
    nb, length = plan["nb"], plan["length"]
    j = i // 2
    if i % 2 == 0:
        return _conformer(x, pre, acc, j, p["gains"], i, p["a_w_in"], p["a_b_in"], p["a_w_dw"], p["a_b_dw"],
                          p["a_ln_g"], p["a_ln_b"], p["a_w_out"], p["a_b_out"], nb, length, plan["conf_rc"],
                          permute_in=(group == "prompt" and i == 0))
    return _pool(x, pre, acc, j, pos0, p["gains"], i, p["p_w_group"], p["p_scale"], nb, length)


def _conv_ffn(group, x, i, ffn_pre, acc, p):
    plan = _PLAN[group]
    return _ffn(x, ffn_pre, acc, p["gains"], i, p["f_w_up"], p["f_w_dw"], p["f_b_dw"], p["f_w_down"], plan["nb"],
                plan["ffn_length"], permute_out=(group == "prompt" and i == DEPTH - 1))


def _cache_rows(cache):
    d, b = cache.shape[:2]
    lane_tiles = MEM_HEAD_DIM // LANES
    tiled = cache.reshape(d, b, N_MEM, N_MEM_HEADS, lane_tiles, LANES).transpose(0, 1, 2, 4, 3, 5)
    return tiled.reshape(d, b, N_MEM * lane_tiles * N_MEM_HEADS, LANES)


def _cache_unrows(rows):
    d, b = rows.shape[:2]
    lane_tiles = MEM_HEAD_DIM // LANES
    tiled = rows.reshape(d, b, N_MEM, lane_tiles, N_MEM_HEADS, LANES).transpose(0, 1, 2, 4, 3, 5)
    return tiled.reshape(d, b, N_MEM, N_MEM_HEADS, MEM_HEAD_DIM)


def kernel(x_prompt, x_sample, mem_prompt, state_conv, state_pool, state_ffn, cache_mem_k, cache_mem_v, norm_gains, a_w_in, a_b_in, a_w_dw, a_b_dw, a_ln_g, a_ln_b, a_w_out, a_b_out, p_w_group, p_scale, c_w_q, c_w_kv, c_w_o, f_w_up, f_w_dw, f_b_dw, f_w_down):
    b = x_prompt.shape[0]
    c = D_MODEL
    mk, mv, w_q_b, w_o_b, w_up_b, w_down_b = _mem_kv(mem_prompt, norm_gains, c_w_kv,
                                                     [c_w_q, c_w_o, f_w_up, f_w_down], 1)
    p = dict(
        gains=norm_gains,
        a_w_in=a_w_in.astype(BF16), a_b_in=a_b_in[:, None, :], a_w_dw=a_w_dw, a_b_dw=a_b_dw[:, None, :],
        a_ln_g=a_ln_g[:, None, :], a_ln_b=a_ln_b[:, None, :], a_w_out=a_w_out.astype(BF16),
        a_b_out=a_b_out[:, None, :],
        p_w_group=p_w_group.astype(BF16), p_scale=p_scale[:, None, :],
        c_w_q=w_q_b, c_w_o=w_o_b,
        f_w_up=w_up_b, f_w_dw=f_w_dw, f_b_dw=f_b_dw[:, None, :], f_w_down=w_down_b,
    )
    ck, cv = _cache_rows(cache_mem_k), _cache_rows(cache_mem_v)
    tm = lambda a: a.transpose(0, 2, 1, 3)
    conv_pre, pool_pre = tm(state_conv), tm(state_pool)
    xp, xs = x_prompt, x_sample
    mix_p, ffn_p = [], []
    mix_pre = (conv_pre, pool_pre)
    mix_s = [None, None]
    ffn_s = None
    for i in range(DEPTH):
        kind = i % 2
        xp, st = _mixer("prompt", xp, i, 0, None, None, p)
        mix_p.append(st)
        xs, mix_s[kind] = _mixer("sample", xs, i, PAST_LEN, mix_pre[kind], mix_s[kind], p)
        xp, xs = _attn(xp, xs, mk, mv, ck, cv, norm_gains, i, p["c_w_q"], p["c_w_o"], PROMPT_TILE)
        xp, st = _conv_ffn("prompt", xp, i, None, None, p)
        ffn_p.append(st)
        xs, ffn_s = _conv_ffn("sample", xs, i, state_ffn, ffn_s, p)
    return (xp, xs, jnp.stack(mix_p[0::2]), jnp.stack(mix_p[1::2]), jnp.stack(ffn_p),
            _cache_unrows(mk), _cache_unrows(mv), tm(mix_s[0]), tm(mix_s[1]), ffn_s)
```

```python
import functools

import jax
import jax.numpy as jnp
from jax import lax
from jax.experimental import pallas as pl
from jax.experimental.pallas import tpu as pltpu

D_MODEL = 1024
DEPTH = 4
PAST_LEN = 16384
CONV_WIDTH = 31
CONV_PREFIX = CONV_WIDTH - 1
POOL_WINDOWS = (2, 4, 8, 16)
POOL_GROUP_DIM = D_MODEL // len(POOL_WINDOWS)
POOL_PREFIX = max(POOL_WINDOWS) - 1
N_MEM = 256
N_MEM_HEADS = 4
MEM_HEAD_DIM = D_MODEL // N_MEM_HEADS
D_FF = 2816
FFN_CONV_WIDTH = 3
FFN_PREFIX = FFN_CONV_WIDTH - 1
N_NORMS = 7
RMS_EPS = 1e-6
LN_EPS = 1e-5

SUBLANES = 8
LANES = 128
PROMPT_TILE = 512
PROMPT_SEG = PROMPT_TILE // SUBLANES
FFN_CHUNK = 256
CONV_COLS = LANES
VMEM_LIMIT_BYTES = 56 * 1024 * 1024

BF16 = jnp.bfloat16
F32 = jnp.float32


def _rms(x, g):
    ms = jnp.mean(x * x, axis=-1, keepdims=True)
    return x * lax.rsqrt(ms + RMS_EPS) * g


def _bdot(a, w):
    return jnp.dot(a.astype(BF16), w, preferred_element_type=F32)


def _sigmoid(x):
    return 0.5 * jnp.tanh(0.5 * x) + 0.5


def _to_segments(x):
    rows, c = x.shape
    return x.reshape(SUBLANES, rows // SUBLANES, c).swapaxes(0, 1).reshape(rows, c)


def _from_segments(x):
    rows, c = x.shape
    tiles = [x[r:r + PROMPT_TILE].reshape(PROMPT_SEG, SUBLANES, c).swapaxes(0, 1).reshape(PROMPT_TILE, c)
             for r in range(0, rows, PROMPT_TILE)]
    return tiles[0] if len(tiles) == 1 else jnp.concatenate(tiles, axis=0)


def _load_rows(x_ref, sample, permute_in):
    if sample:
        return jnp.concatenate([x_ref[:, t, :] for t in range(x_ref.shape[1])], axis=0)
    x = x_ref[...]
    return _to_segments(x) if permute_in else x


def _store_rows(y_ref, y, sample, permute_out):
    if sample:
        nb = y_ref.shape[0]
        for t in range(y_ref.shape[1]):
            y_ref[:, t, :] = y[t * nb:(t + 1) * nb]
    else:
        y_ref[...] = _from_segments(y) if permute_out else y


def _segment_history(last, carry_val):
    sub = lax.broadcasted_iota(jnp.int32, last.shape, 1)
    return pltpu.roll(jnp.where(sub == SUBLANES - 1, carry_val, last), 1, axis=1)


def _fill_history(ext, cur3, hist, carry, pre_ref, t, sample):
    n_cur = cur3.shape[0]
    if sample:
        ext[0:hist] = pre_ref[...]
    else:
        @pl.when(t == 0)
        def _():
            carry[...] = jnp.zeros(carry.shape, F32)

        last = cur3[n_cur - hist:]
        ext[0:hist] = _segment_history(last, carry[...])
        carry[...] = last
    ext[hist:hist + n_cur] = cur3


def _store_state(st_ref, ext, hist, carry, t, nt, sample):
    n_cur = ext.shape[0] - hist
    if sample:
        st_ref[...] = ext[n_cur:n_cur + hist]
    else:
        @pl.when(t == nt - 1)
        def _():
            st_ref[...] = carry[:, SUBLANES - 1, :]


def _split_refs(refs, sample, n_in, n_out, n_scratch):
    refs = list(refs)
    x_ref = refs.pop(0)
    pre_ref = refs.pop(0) if sample else None
    ins, refs = refs[:n_in], refs[n_in:]
    if sample == "stacked":
        refs.pop(0)
    outs, refs = refs[:n_out], refs[n_out:]
    scr, refs = refs[:n_scratch], refs[n_scratch:]
    carry = None if sample else refs.pop(0)
    assert not refs
    return x_ref, pre_ref, ins, outs, scr, carry


def _conformer_kernel(sample, permute_in, rc, *refs):
    x_ref, pre_ref, ins, outs, scr, carry = _split_refs(refs, sample, 9, 2, 2)
    g_ref, w_in, b_in, w_dw, b_dw, ln_g, ln_b, w_out, b_out = ins
    y_ref, st_ref = outs
    ext, cbuf = scr
    t = pl.program_id(1)
    nt = pl.num_programs(1)
    n_cur, r8, c = cbuf.shape
    rows = n_cur * r8

    x = _load_rows(x_ref, sample, permute_in)
    h = _rms(x, g_ref[0:1, :])
    u = _bdot(h, w_in[...]) + b_in[...]
    glu = u[:, :c] * _sigmoid(u[:, c:])
    _fill_history(ext, glu.reshape(n_cur, r8, c), CONV_PREFIX, carry, pre_ref, t, sample)

    def chunk(ci, carry_):
        i0 = ci * rc
        for g in range(r8 // SUBLANES):
            rsl = slice(g * SUBLANES, (g + 1) * SUBLANES)
            def lane_col(l, carry2):
                cols = pl.ds(pl.multiple_of(l * CONV_COLS, CONV_COLS), CONV_COLS)
                bias = jnp.broadcast_to(b_dw[:, cols], (SUBLANES, CONV_COLS))
                accs = [bias] * rc
                for m in range(rc + CONV_PREFIX):
                    tile = ext[i0 + m, rsl, cols]
                    for j in range(max(0, m - CONV_PREFIX), min(rc, m + 1)):
                        accs[j] = accs[j] + w_dw[m - j:m - j + 1, cols] * tile
                cbuf[pl.ds(i0, rc), rsl, cols] = jnp.stack(accs)
                return carry2

            lax.fori_loop(0, c // CONV_COLS, lane_col, 0)
        return carry_

    lax.fori_loop(0, n_cur // rc, chunk, 0)

    v = cbuf[...].reshape(rows, c)
    mu = jnp.mean(v, axis=-1, keepdims=True)
    xc = v - mu
    var = jnp.mean(xc * xc, axis=-1, keepdims=True)
    yn = xc * lax.rsqrt(var + LN_EPS) * ln_g[...] + ln_b[...]
    out = _bdot(yn * _sigmoid(yn), w_out[...]) + b_out[...]
    _store_rows(y_ref, x + _rms(out, g_ref[1:2, :]), sample, False)
    _store_state(st_ref, ext, CONV_PREFIX, carry, t, nt, sample)


def _x_specs(sample, nb, length, c):
    if sample:
        return pl.BlockSpec((nb, length, c), lambda bb, tt: (bb, 0, 0))
    return pl.BlockSpec((None, length, c), lambda bb, tt: (bb, tt, 0))


def _state_io(prefix, acc, layer, block, index, b, hist, width, in_specs, args):
    if prefix is None:
        spec = pl.BlockSpec((None, hist, width), lambda bb, tt: (bb, 0, 0))
        return False, spec, jax.ShapeDtypeStruct((b, hist, width), F32), {}
    spec = pl.BlockSpec((None,) + block, lambda bb, tt: (layer,) + index(bb))
    shape = jax.ShapeDtypeStruct(prefix.shape, F32)
    in_specs.append(pl.BlockSpec(memory_space=pl.ANY))
    args.append(jnp.zeros(prefix.shape, F32) if acc is None else acc)
    return "stacked", spec, shape, {len(args) - 1: 1}


def _time_tiles(sample, nb, length):
    return (length, nb) if sample else (length // SUBLANES, SUBLANES)


def _params(sem=("arbitrary", "arbitrary")):
    return pltpu.CompilerParams(dimension_semantics=sem, vmem_limit_bytes=VMEM_LIMIT_BYTES)


def _conformer(x, prefix, acc, j, gains, i, w_in, b_in, w_dw, b_dw, ln_g, ln_b, w_out, b_out, nb, length, rc,
               permute_in):
    b, s, c = x.shape
    sample = prefix is not None
    assert (s == length) if sample else (nb == 1)
    n_cur, r8 = _time_tiles(sample, nb, length)
    lay = lambda bb, tt: (i, 0, 0)
    layj = lambda bb, tt: (j, 0, 0)
    single = pl.Buffered(1)
    in_specs = [_x_specs(sample, nb, length, c)]
    args = [x]
    if sample:
        in_specs.append(pl.BlockSpec((None, CONV_PREFIX, nb, c), lambda bb, tt: (j, 0, bb, 0)))
        args.append(prefix)
    in_specs += [
        pl.BlockSpec((None, N_NORMS, c), lay),
        pl.BlockSpec((None, c, 2 * c), layj, pipeline_mode=single),
        pl.BlockSpec((None, 1, 2 * c), layj),
        pl.BlockSpec((None, CONV_WIDTH, c), layj),
        pl.BlockSpec((None, 1, c), layj),
        pl.BlockSpec((None, 1, c), layj),
        pl.BlockSpec((None, 1, c), layj),
        pl.BlockSpec((None, c, c), layj, pipeline_mode=single),
        pl.BlockSpec((None, 1, c), layj),
    ]
    args += [gains, w_in, b_in, w_dw, b_dw, ln_g, ln_b, w_out, b_out]
    mode, st_spec, st_shape, aliases = _state_io(prefix, acc, j, (CONV_PREFIX, nb, c), lambda bb: (0, bb, 0),
                                                 b, CONV_PREFIX, c, in_specs, args)
    scratch = [pltpu.VMEM((CONV_PREFIX + n_cur, r8, c), F32), pltpu.VMEM((n_cur, r8, c), F32)]
    if not sample:
        scratch.append(pltpu.VMEM((CONV_PREFIX, SUBLANES, c), F32))
    return pl.pallas_call(
        functools.partial(_conformer_kernel, mode, permute_in, rc),
        grid=(b // nb, s // length),
        in_specs=in_specs,
        out_specs=[_x_specs(sample, nb, length, c), st_spec],
        out_shape=[jax.ShapeDtypeStruct((b, s, c), F32), st_shape],
        scratch_shapes=scratch,
        input_output_aliases=aliases,
        compiler_params=_params(),
        name="conformer_mixer",
    )(*args)


def _pool_kernel(sample, pos0, *refs):
    x_ref, pre_ref, ins, outs, _, carry = _split_refs(refs, sample, 3, 2, 0)
    g_ref, w_grp, scale = ins
    y_ref, st_ref = outs
    t = pl.program_id(1)
    nt = pl.num_programs(1)
    if sample:
        (r8, n_cur), n_sub = x_ref.shape[:2], 1
    else:
        n_cur, r8, n_sub = PROMPT_SEG, SUBLANES, x_ref.shape[0] // PROMPT_TILE
    c = x_ref.shape[-1]
    rows = n_cur * r8
    gd = POOL_GROUP_DIM
    hist = POOL_PREFIX

    x = _load_rows(x_ref, sample, False)
    h = _rms(x, g_ref[0:1, :])
    if not sample:
        @pl.when(t == 0)
        def _():
            carry[...] = jnp.zeros(carry.shape, F32)

    def pooled_tile(sub, past):
        h3 = h[sub * rows:(sub + 1) * rows].reshape(n_cur, r8, c)
        if past is None:
            past = _segment_history(h3[n_cur - hist:], prev[0])
            prev[0] = h3[n_cur - hist:]
        e = jnp.concatenate([past, h3], axis=0)
        step = lax.broadcasted_iota(jnp.int32, (n_cur, r8, 1), 0)
        if sample:
            pos = pos0 + step
        else:
            seg = lax.broadcasted_iota(jnp.int32, (n_cur, r8, 1), 1)
            pos = pos0 + (t * n_sub + sub) * rows + seg * n_cur + step
        posf = pos.astype(F32)
        parts = []
        for gi, w in enumerate(POOL_WINDOWS):
            eg = e[:, :, gi * gd:(gi + 1) * gd]
            s, m = eg, 1
            while m < w:
                s = s[m:] + s[:-m]
                m *= 2
            first = hist - (w - 1)
            inv = 1.0 / jnp.minimum(jnp.float32(w), posf + 1.0)
            parts.append((s[first:first + n_cur] * inv - eg[hist:]).reshape(rows, gd))
        return jnp.concatenate(parts, axis=-1), e

    if sample:
        pooled, e = pooled_tile(0, pre_ref[...])
    else:
        prev = [carry[...]]
        tiles = [pooled_tile(sub, None)[0] for sub in range(n_sub)]
        carry[...] = prev[0]
        pooled = tiles[0] if n_sub == 1 else jnp.concatenate(tiles, axis=0)
    outs_g = [_bdot(pooled[:, gi * gd:(gi + 1) * gd], w_grp[gi]) for gi in range(len(POOL_WINDOWS))]
    out = jnp.concatenate(outs_g, axis=-1) * scale[...]
    _store_rows(y_ref, x + _rms(out, g_ref[1:2, :]), sample, False)
    if sample:
        st_ref[...] = e[n_cur:]
    else:
        @pl.when(t == nt - 1)
        def _():
            st_ref[...] = carry[:, SUBLANES - 1, :]


def _pool(x, prefix, acc, j, pos0, gains, i, w_grp, scale, nb, length):
    b, s, c = x.shape
    sample = prefix is not None
    assert (s == length) if sample else (nb == 1)
    in_specs = [_x_specs(sample, nb, length, c)]
    args = [x]
    if sample:
        in_specs.append(pl.BlockSpec((None, POOL_PREFIX, nb, c), lambda bb, tt: (j, 0, bb, 0)))
        args.append(prefix)
    in_specs += [
        pl.BlockSpec((None, N_NORMS, c), lambda bb, tt: (i, 0, 0)),
        pl.BlockSpec((None, len(POOL_WINDOWS), POOL_GROUP_DIM, POOL_GROUP_DIM), lambda bb, tt: (j, 0, 0, 0)),
        pl.BlockSpec((None, 1, c), lambda bb, tt: (j, 0, 0)),
    ]
    args += [gains, w_grp, scale]
    mode, st_spec, st_shape, aliases = _state_io(prefix, acc, j, (POOL_PREFIX, nb, c), lambda bb: (0, bb, 0),
                                                 b, POOL_PREFIX, c, in_specs, args)
    scratch = [] if sample else [pltpu.VMEM((POOL_PREFIX, SUBLANES, c), F32)]
    return pl.pallas_call(
        functools.partial(_pool_kernel, mode, pos0),
        grid=(b // nb, s // length),
        in_specs=in_specs,
        out_specs=[_x_specs(sample, nb, length, c), st_spec],
        out_shape=[jax.ShapeDtypeStruct((b, s, c), F32), st_shape],
        scratch_shapes=scratch,
        input_output_aliases=aliases,
        compiler_params=_params(),
        name="pool_mixer",
    )(*args)


def _attend(qh, kh, vh):
    s = lax.dot_general(qh, kh.astype(BF16), (((1,), (1,)), ((), ())), preferred_element_type=F32)
    e = jnp.exp(s - jnp.max(s, axis=-1, keepdims=True))
    p = e * (1.0 / jnp.sum(e, axis=-1, keepdims=True))
    return jnp.dot(p.astype(BF16), vh.astype(BF16), preferred_element_type=F32)


def _cache_head(ref, hh):
    lane_tiles = MEM_HEAD_DIM // LANES
    stride = lane_tiles * N_MEM_HEADS
    parts = [ref[pl.ds(dt * N_MEM_HEADS + hh, N_MEM, stride=stride), :] for dt in range(lane_tiles)]
    return jnp.concatenate(parts, axis=-1)


def _attn_kernel(xp_ref, xs_ref, kp_ref, vp_ref, ks_ref, vs_ref, g_ref, w_q, w_o, yp_ref, ys_ref, obuf):
    rows_p, c = xp_ref.shape
    nb, length, _ = xs_ref.shape
    hd = MEM_HEAD_DIM
    nh = N_MEM_HEADS
    x = jnp.concatenate([xp_ref[...], xs_ref[...].reshape(nb * length, c)], axis=0)
    q = _bdot(_rms(x, g_ref[2:3, :]), w_q[...]) * (hd ** -0.5)

    qp = q[:rows_p].astype(BF16)
    for hh in range(nh):
        cols = slice(hh * hd, (hh + 1) * hd)
        obuf[0:rows_p, cols] = _attend(qp[:, cols], _cache_head(kp_ref, hh), _cache_head(vp_ref, hh))

    qs = q[rows_p:]
    col_head = lax.broadcasted_iota(jnp.int32, qs.shape, 1) // hd
    q_heads = [jnp.where(col_head == hh, qs, 0.0).astype(BF16) for hh in range(nh)]
    scores = []
    for n in range(nb):
        rsl = slice(n * length, (n + 1) * length)
        qn = jnp.concatenate([qh[rsl] for qh in q_heads], axis=0)
        kn = jnp.concatenate([_cache_head(ks_ref.at[n], hh) for hh in range(nh)], axis=-1).astype(BF16)
        scores.append(lax.dot_general(qn, kn, (((1,), (1,)), ((), ())), preferred_element_type=F32))
    s = jnp.concatenate(scores, axis=0)
    e = jnp.exp(s - jnp.max(s, axis=-1, keepdims=True))
    p = (e * (1.0 / jnp.sum(e, axis=-1, keepdims=True))).astype(BF16)
    for n in range(nb):
        vn = jnp.concatenate([_cache_head(vs_ref.at[n], hh) for hh in range(nh)], axis=-1).astype(BF16)
        o = jnp.dot(p[n * nh * length:(n + 1) * nh * length], vn, preferred_element_type=F32)
        for hh in range(nh):
            obuf[rows_p + n * length:rows_p + (n + 1) * length, hh * hd:(hh + 1) * hd] = (
                o[hh * length:(hh + 1) * length, hh * hd:(hh + 1) * hd])

    y = x + _rms(_bdot(obuf[...], w_o[...]), g_ref[3:4, :])
    yp_ref[...] = y[:rows_p]
    ys_ref[...] = y[rows_p:].reshape(nb, length, c)


def _attn(xp, xs, mem_k, mem_v, cache_k, cache_v, gains, i, w_q, w_o, length):
    b, s, c = xp.shape
    db, ls, _ = xs.shape
    nt = s // length
    nb = db // (b * nt)
    assert nb * b * nt == db
    lay = lambda bb, tt: (i, 0, 0)
    crow = mem_k.shape[2:]
    p_kv = pl.BlockSpec((None, None) + crow, lambda bb, tt: (i, bb, 0, 0))
    s_kv = pl.BlockSpec((None, nb) + crow, lambda bb, tt: (i, bb * nt + tt, 0, 0))
    xp_spec = pl.BlockSpec((None, length, c), lambda bb, tt: (bb, tt, 0))
    xs_spec = pl.BlockSpec((nb, ls, c), lambda bb, tt: (bb * nt + tt, 0, 0))
    return pl.pallas_call(
        _attn_kernel,
        grid=(b, nt),
        in_specs=[
            xp_spec, xs_spec, p_kv, p_kv, s_kv, s_kv,
            pl.BlockSpec((None, N_NORMS, c), lay),
            pl.BlockSpec((None, c, c), lay, pipeline_mode=pl.Buffered(1)),
            pl.BlockSpec((None, c, c), lay, pipeline_mode=pl.Buffered(1)),
        ],
        out_specs=[xp_spec, xs_spec],
        out_shape=[jax.ShapeDtypeStruct(xp.shape, F32), jax.ShapeDtypeStruct(xs.shape, F32)],
        scratch_shapes=[pltpu.VMEM((length + nb * ls, c), F32)],
        compiler_params=_params(),
        name="mem_cross_attn",
    )(xp, xs, mem_k, mem_v, cache_k, cache_v, gains, w_q, w_o)


def _ffn_kernel(sample, permute_out, *refs):
    x_ref, pre_ref, ins, outs, (act,), carry = _split_refs(refs, sample, 5, 2, 1)
    g_ref, w_up, w_dw, b_dw, w_down = ins
    y_ref, st_ref = outs
    t = pl.program_id(1)
    nt = pl.num_programs(1)
    if sample:
        (r8, n_cur), n_sub = x_ref.shape[:2], 1
    else:
        n_cur, r8, n_sub = PROMPT_SEG, SUBLANES, x_ref.shape[0] // PROMPT_TILE
    rows = n_cur * r8
    c = x_ref.shape[-1]
    cw = FFN_CHUNK
    hist = FFN_PREFIX

    x = _load_rows(x_ref, sample, False)
    hb = _rms(x, g_ref[4:5, :]).astype(BF16)

    if not sample:
        @pl.when(t == 0)
        def _():
            carry[...] = jnp.zeros(carry.shape, F32)

    def taps(u, past, cols):
        e = jnp.concatenate([past.reshape(hist * r8, cw), u], axis=0)
        y = b_dw[:, cols] + w_dw[FFN_CONV_WIDTH - 1:FFN_CONV_WIDTH, cols] * u
        for k in range(hist):
            y = y + w_dw[k:k + 1, cols] * e[k * r8:k * r8 + rows]
        return y

    def conv(cols):
        u = jnp.dot(hb, w_up[:, cols], preferred_element_type=F32)
        if sample:
            u3 = u.reshape(n_cur, r8, cw)
            for j in range(hist):
                st_ref[:, j, cols] = u3[n_cur - hist + j]
            return taps(u, jnp.stack([pre_ref[:, j, cols] for j in range(hist)]), cols)
        prev = carry[:, :, cols]
        ys = []
        for sub in range(n_sub):
            us = u[sub * rows:(sub + 1) * rows]
            last = us.reshape(n_cur, r8, cw)[n_cur - hist:]
            ys.append(taps(us, _segment_history(last, prev), cols))
            prev = last
        carry[:, :, cols] = prev
        return ys[0] if n_sub == 1 else jnp.concatenate(ys, axis=0)

    for ci in range(D_FF // cw):
        gcols = slice(ci * cw, (ci + 1) * cw)
        vcols = slice(D_FF + ci * cw, D_FF + (ci + 1) * cw)
        cg = conv(gcols)
        cv = conv(vcols)
        act[:, gcols] = (cg * _sigmoid(cg) * cv).astype(BF16)
    out = jnp.dot(act[...], w_down[...], preferred_element_type=F32)
    _store_rows(y_ref, x + _rms(out, g_ref[5:6, :]), sample, permute_out)

    if not sample:
        @pl.when(t == nt - 1)
        def _():
            st_ref[...] = carry[:, SUBLANES - 1, :]


def _ffn(x, prefix, acc, gains, i, w_up, w_dw, b_dw, w_down, nb, length, permute_out):
    b, s, c = x.shape
    f2 = 2 * D_FF
    sample = prefix is not None
    assert (s == length) if sample else (nb == 1)
    lay = lambda bb, tt: (i, 0, 0)
    single = pl.Buffered(1)
    in_specs = [_x_specs(sample, nb, length, c)]
    args = [x]
    if sample:
        in_specs.append(pl.BlockSpec((None, nb, FFN_PREFIX, f2), lambda bb, tt: (i, bb, 0, 0)))
        args.append(prefix)
    in_specs += [
        pl.BlockSpec((None, N_NORMS, c), lay),
        pl.BlockSpec((None, c, f2), lay, pipeline_mode=single),
        pl.BlockSpec((None, FFN_CONV_WIDTH, f2), lay),
        pl.BlockSpec((None, 1, f2), lay),
        pl.BlockSpec((None, D_FF, c), lay, pipeline_mode=single),
    ]
    args += [gains, w_up, w_dw, b_dw, w_down]
    mode, st_spec, st_shape, aliases = _state_io(prefix, acc, i, (nb, FFN_PREFIX, f2), lambda bb: (bb, 0, 0),
                                                 b, FFN_PREFIX, f2, in_specs, args)
    scratch = [pltpu.VMEM((nb * length, D_FF), BF16)]
    if not sample:
        scratch.append(pltpu.VMEM((FFN_PREFIX, SUBLANES, f2), F32))
    return pl.pallas_call(
        functools.partial(_ffn_kernel, mode, permute_out),
        grid=(b // nb, s // length),
        in_specs=in_specs,
        out_specs=[_x_specs(sample, nb, length, c), st_spec],
        out_shape=[jax.ShapeDtypeStruct((b, s, c), F32), st_shape],
        scratch_shapes=scratch,
        input_output_aliases=aliases,
        compiler_params=_params(),
        name="conv_ffn",
    )(*args)


def _mem_kv_kernel(n_cast, m_ref, g_ref, w_kv, *refs):
    src, (k_ref, v_ref), dst = refs[:n_cast], refs[n_cast:n_cast + 2], refs[n_cast + 2:]
    for s_ref, d_ref in zip(src, dst):
        d_ref[...] = s_ref[...].astype(BF16)
    nbm, n_mem, c = m_ref.shape
    kv = _bdot(_rms(m_ref[...].reshape(nbm * n_mem, c), g_ref[N_NORMS - 1:N_NORMS, :]), w_kv[...].astype(BF16))
    lane_tiles = MEM_HEAD_DIM // LANES
    stride = lane_tiles * N_MEM_HEADS
    for ref, base in ((k_ref, 0), (v_ref, c)):
        for bi in range(nbm):
            for hh in range(N_MEM_HEADS):
                for dt in range(lane_tiles):
                    col = base + hh * MEM_HEAD_DIM + dt * LANES
                    ref[bi, pl.ds(dt * N_MEM_HEADS + hh, n_mem, stride=stride), :] = (
                        kv[bi * n_mem:(bi + 1) * n_mem, col:col + LANES])


def _mem_kv(mem, gains, w_kv, to_cast, nbm):
    b, n_mem, c = mem.shape
    crows = n_mem * (c // LANES)
    steps = b // nbm
    out_spec = pl.BlockSpec((None, nbm, crows, LANES), lambda ii, rr: (ii, rr, 0, 0))
    out_shape = jax.ShapeDtypeStruct((DEPTH, b, crows, LANES), F32)
    slabs = [pl.BlockSpec((None, w.shape[1] // steps, w.shape[2]), lambda ii, rr: (ii, rr, 0)) for w in to_cast]
    return pl.pallas_call(
        functools.partial(_mem_kv_kernel, len(to_cast)),
        grid=(DEPTH, steps),
        in_specs=[
            pl.BlockSpec((nbm, n_mem, c), lambda ii, rr: (rr, 0, 0)),
            pl.BlockSpec((None, N_NORMS, c), lambda ii, rr: (ii, 0, 0)),
            pl.BlockSpec((None, c, 2 * c), lambda ii, rr: (ii, 0, 0), pipeline_mode=pl.Buffered(1)),
        ] + slabs,
        out_specs=[out_spec, out_spec] + slabs,
        out_shape=[out_shape, out_shape] + [jax.ShapeDtypeStruct(w.shape, BF16) for w in to_cast],
        compiler_params=_params(),
        name="mem_kv",
    )(mem, gains, w_kv, *to_cast)


_PLAN = {
    "prompt": dict(nb=1, length=PROMPT_TILE, ffn_length=2 * PROMPT_TILE, pool_length=2 * PROMPT_TILE, conf_rc=32),
    "sample": dict(nb=32, length=8, ffn_length=8, pool_length=8, conf_rc=8),
}


def _mixer(group, x, i, pos0, pre, acc, p):
    plan = _PLAN[group]
    nb, length = plan["nb"], plan["length"]
    j = i // 2
    if i % 2 == 0:
        return _conformer(x, pre, acc, j, p["gains"], i, p["a_w_in"], p["a_b_in"], p["a_w_dw"], p["a_b_dw"],
                          p["a_ln_g"], p["a_ln_b"], p["a_w_out"], p["a_b_out"], nb, length, plan["conf_rc"],
                          permute_in=(group == "prompt" and i == 0))
    return _pool(x, pre, acc, j, pos0, p["gains"], i, p["p_w_group"], p["p_scale"], nb, plan["pool_length"])


def _conv_ffn(group, x, i, ffn_pre, acc, p):
    plan = _PLAN[group]
    return _ffn(x, ffn_pre, acc, p["gains"], i, p["f_w_up"], p["f_w_dw"], p["f_b_dw"], p["f_w_down"], plan["nb"],
                plan["ffn_length"], permute_out=(group == "prompt" and i == DEPTH - 1))


def _cache_rows(cache):
    d, b = cache.shape[:2]
    lane_tiles = MEM_HEAD_DIM // LANES
    tiled = cache.reshape(d, b, N_MEM, N_MEM_HEADS, lane_tiles, LANES).transpose(0, 1, 2, 4, 3, 5)
    return tiled.reshape(d, b, N_MEM * lane_tiles * N_MEM_HEADS, LANES)


def _cache_unrows(rows):
    d, b = rows.shape[:2]
    lane_tiles = MEM_HEAD_DIM // LANES
    tiled = rows.reshape(d, b, N_MEM, lane_tiles, N_MEM_HEADS, LANES).transpose(0, 1, 2, 4, 3, 5)
    return tiled.reshape(d, b, N_MEM, N_MEM_HEADS, MEM_HEAD_DIM)


def kernel(x_prompt, x_sample, mem_prompt, state_conv, state_pool, state_ffn, cache_mem_k, cache_mem_v, norm_gains, a_w_in, a_b_in, a_w_dw, a_b_dw, a_ln_g, a_ln_b, a_w_out, a_b_out, p_w_group, p_scale, c_w_q, c_w_kv, c_w_o, f_w_up, f_w_dw, f_b_dw, f_w_down):
    b = x_prompt.shape[0]
    c = D_MODEL
    mk, mv, w_q_b, w_o_b, w_up_b, w_down_b = _mem_kv(mem_prompt, norm_gains, c_w_kv,
                                                     [c_w_q, c_w_o, f_w_up, f_w_down], 1)
    p = dict(
        gains=norm_gains,
        a_w_in=a_w_in.astype(BF16), a_b_in=a_b_in[:, None, :], a_w_dw=a_w_dw, a_b_dw=a_b_dw[:, None, :],
        a_ln_g=a_ln_g[:, None, :], a_ln_b=a_ln_b[:, None, :], a_w_out=a_w_out.astype(BF16),
        a_b_out=a_b_out[:, None, :],
        p_w_group=p_w_group.astype(BF16), p_scale=p_scale[:, None, :],
        c_w_q=w_q_b, c_w_o=w_o_b,
        f_w_up=w_up_b, f_w_dw=f_w_dw, f_b_dw=f_b_dw[:, None, :], f_w_down=w_down_b,
    )
    ck, cv = _cache_rows(cache_mem_k), _cache_rows(cache_mem_v)
    tm = lambda a: a.transpose(0, 2, 1, 3)
    conv_pre, pool_pre = tm(state_conv), tm(state_pool)
    xp, xs = x_prompt, x_sample
    mix_p, ffn_p = [], []
    mix_pre = (conv_pre, pool_pre)
    mix_s = [None, None]
    ffn_s = None
    for i in range(DEPTH):
        kind = i % 2
        xp, st = _mixer("prompt", xp, i, 0, None, None, p)
        mix_p.append(st)
        xs, mix_s[kind] = _mixer("sample", xs, i, PAST_LEN, mix_pre[kind], mix_s[kind], p)
        xp, xs = _attn(xp, xs, mk, mv, ck, cv, norm_gains, i, p["c_w_q"], p["c_w_o"], PROMPT_TILE)
        xp, st = _conv_ffn("prompt", xp, i, None, None, p)
        ffn_p.append(st)
        xs, ffn_s = _conv_ffn("sample", xs, i, state_ffn, ffn_s, p)
    return (xp, xs, jnp.stack(mix_p[0::2]), jnp.stack(mix_p[1::2]), jnp.stack(ffn_p),
            _cache_unrows(mk), _cache_unrows(mv), tm(mix_s[0]), tm(mix_s[1]), ffn_s)
```
